```python
import jax, jax.numpy as jnp
from jax import lax
import numpy as np

D_MODEL = 1024
BATCH = 8
SEQ = 4096
DEPTH = 4
DEC_BATCH = 4
DEC_SEQ = 8192
PAST_LEN = 128

N_HEADS = 8
QK_NOPE_DIM = 64
QK_ROPE_DIM = 32
V_HEAD_DIM = 64
QK_DIM = QK_NOPE_DIM + QK_ROPE_DIM
Q_LORA_RANK = 384
KV_LORA_RANK = 256
ATTN_WIDTH = N_HEADS * V_HEAD_DIM
F_GROUPS = 8
F_GROUP_DIM = 64
F_WIDTH = F_GROUPS * F_GROUP_DIM
IN_WIDTH = Q_LORA_RANK + KV_LORA_RANK + QK_ROPE_DIM + F_WIDTH + 2 * D_MODEL
N_EXPERTS = 32
TOP_K = 4
D_FF = D_MODEL
SWIGLU_LIMIT = 7.0
SWIGLU_ALPHA = 1.702
ROPE_THETA = 10000.0
RMS_EPS = 1e-6
Q_BLOCK = 128
ROW_BLOCK = 256
N_MOD = 6

kernel_name = 'hybrid_mla_fnet_moe_adaln_encoder'


def rmsnorm(x, g):
    xf = x.astype(jnp.float32)
    xf = xf * lax.rsqrt(jnp.mean(xf * xf, axis=-1, keepdims=True) + RMS_EPS)
    return xf.astype(x.dtype) * g


def rope_tables(seq, dtype):
    inv = 1.0 / (ROPE_THETA ** (jnp.arange(0, QK_ROPE_DIM, 2, dtype=jnp.float32) / QK_ROPE_DIM))
    ang = jnp.arange(seq, dtype=jnp.float32)[:, None] * inv[None, :]
    return jnp.cos(ang).astype(dtype), jnp.sin(ang).astype(dtype)


def apply_rope(t, cos, sin):
    half = QK_ROPE_DIM // 2
    t1, t2 = t[..., :half], t[..., half:]
    c = cos[None, :, None, :]
    s = sin[None, :, None, :]
    return jnp.concatenate([t1 * c - t2 * s, t1 * s + t2 * c], axis=-1)


def mla_branch(u_q, u_kv, u_kr, g_q, w_uq, g_kv, w_ukv):
    bsz, seq, _ = u_q.shape
    cos, sin = rope_tables(seq, u_q.dtype)
    q = (rmsnorm(u_q, g_q) @ w_uq).reshape(bsz, seq, N_HEADS, QK_DIM)
    q = jnp.concatenate([q[..., :QK_NOPE_DIM], apply_rope(q[..., QK_NOPE_DIM:], cos, sin)], axis=-1)
    q = q * (QK_DIM ** -0.5)
    kv = (rmsnorm(u_kv, g_kv) @ w_ukv).reshape(bsz, seq, N_HEADS, QK_NOPE_DIM + V_HEAD_DIM)
    k_nope, v = kv[..., :QK_NOPE_DIM], kv[..., QK_NOPE_DIM:]
    k_rope = apply_rope(u_kr[:, :, None, :], cos, sin)
    k = jnp.concatenate([k_nope, jnp.broadcast_to(k_rope, (bsz, seq, N_HEADS, QK_ROPE_DIM))], axis=-1)
    n_blk = seq // Q_BLOCK
    qb = q.reshape(bsz, n_blk, Q_BLOCK, N_HEADS, QK_DIM).transpose(1, 0, 2, 3, 4)

    def attend(q_blk):
        s = jnp.einsum('bqhd,bkhd->bhqk', q_blk, k).astype(jnp.float32)
        p = jax.nn.softmax(s, axis=-1).astype(v.dtype)
        return jnp.einsum('bhqk,bkhd->bqhd', p, v)

    o = lax.map(attend, qb)
    return o.transpose(1, 0, 2, 3, 4).reshape(bsz, seq, ATTN_WIDTH)


def fourier_branch(u_f):
    bsz, seq, _ = u_f.shape
    z = u_f.astype(jnp.float32).reshape(bsz, seq, F_GROUPS, F_GROUP_DIM)
    y = jnp.fft.fft2(z, axes=(1, 3), norm='ortho').real
    return y.reshape(bsz, seq, F_WIDTH).astype(u_f.dtype)


def moe(h, w_router, b_router, w_gu, b_gu, w_dn, b_dn):
    bsz, seq, d = h.shape
    t = bsz * seq
    hf = h.reshape(t, d)
    logits = (hf @ w_router + b_router).astype(jnp.float32)
    top_val, top_idx = lax.top_k(logits, TOP_K)
    gates = jax.nn.softmax(top_val, axis=-1)
    a = t * TOP_K
    e_flat = top_idx.reshape(a).astype(jnp.int32)
    tok_flat = jnp.arange(a, dtype=jnp.int32) // TOP_K
    g_flat = gates.reshape(a)
    order = jnp.argsort(e_flat)
    e_sorted = e_flat[order]
    counts = jnp.zeros((N_EXPERTS,), jnp.int32).at[e_flat].add(1)
    padded = (counts + ROW_BLOCK - 1) // ROW_BLOCK * ROW_BLOCK
    start = jnp.cumsum(counts) - counts
    pend = jnp.cumsum(padded)
    pstart = pend - padded
    dest = pstart[e_sorted] + (jnp.arange(a, dtype=jnp.int32) - start[e_sorted])
    n_blocks = (a + N_EXPERTS * ROW_BLOCK + ROW_BLOCK - 1) // ROW_BLOCK
    p = n_blocks * ROW_BLOCK
    row_tok = jnp.full((p,), t, jnp.int32).at[dest].set(tok_flat[order])
    row_gate = jnp.zeros((p,), jnp.float32).at[dest].set(g_flat[order])
    blk_start = jnp.arange(n_blocks, dtype=jnp.int32) * ROW_BLOCK
    blk_expert = jnp.minimum(jnp.searchsorted(pend, blk_start, side='right'), N_EXPERTS - 1).astype(jnp.int32)
    xs = jnp.concatenate([hf, jnp.zeros((1, d), hf.dtype)], axis=0)[row_tok].reshape(n_blocks, ROW_BLOCK, d)

    def expert_block(args):
        xb, e = args
        gu = xb @ w_gu[e] + b_gu[e]
        gate = jnp.minimum(gu[..., 0::2], SWIGLU_LIMIT)
        up = jnp.clip(gu[..., 1::2], -SWIGLU_LIMIT, SWIGLU_LIMIT)
        glu = gate * jax.nn.sigmoid(gate * SWIGLU_ALPHA)
        return ((up + 1.0) * glu) @ w_dn[e] + b_dn[e]

    ys = lax.map(expert_block, (xs, blk_expert)).reshape(p, d)
    ys = ys * row_gate[:, None].astype(ys.dtype)
    y = jnp.zeros((t + 1, d), ys.dtype).at[row_tok].add(ys)
    return y[:t].reshape(bsz, seq, d)


def encoder_layer(x, c, w_ada, b_ada, g_mix, g_ffn, w_in, g_q, w_uq, g_kv, w_ukv,
                  w_a, w_b, w_out, w_router, b_router, w_gu, b_gu, w_dn, b_dn):
    mod = jax.nn.silu(c) @ w_ada + b_ada
    sh1, sc1, ga1, sh2, sc2, ga2 = [m[:, None, :] for m in jnp.split(mod, N_MOD, axis=-1)]
    h = rmsnorm(x, g_mix) * (1.0 + sc1) + sh1
    u = h @ w_in
    o1 = Q_LORA_RANK
    o2 = o1 + KV_LORA_RANK
    o3 = o2 + QK_ROPE_DIM
    o4 = o3 + F_WIDTH
    o5 = o4 + D_MODEL
    y_a = mla_branch(u[..., :o1], u[..., o1:o2], u[..., o2:o3], g_q, w_uq, g_kv, w_ukv) @ w_a
    y_b = fourier_branch(u[..., o3:o4]) @ w_b
    merged = jax.nn.sigmoid(u[..., o4:o5]) * y_a + jax.nn.sigmoid(u[..., o5:]) * y_b
    x = x + ga1 * (merged @ w_out)
    h = rmsnorm(x, g_ffn) * (1.0 + sc2) + sh2
    return x + ga2 * moe(h, w_router, b_router, w_gu, b_gu, w_dn, b_dn)


def trunk(x, c, w_ada, b_ada, g_mix, g_ffn, w_in, g_q, w_uq, g_kv, w_ukv,
          w_a, w_b, w_out, w_router, b_router, w_gu, b_gu, w_dn, b_dn, g_final):
    for l in range(DEPTH):
        x = encoder_layer(x, c, w_ada[l], b_ada[l], g_mix[l], g_ffn[l], w_in[l], g_q[l], w_uq[l],
                          g_kv[l], w_ukv[l], w_a[l], w_b[l], w_out[l], w_router[l], b_router[l],
                          w_gu[l], b_gu[l], w_dn[l], b_dn[l])
    return rmsnorm(x, g_final)


def setup_inputs(seed: int = 0) -> dict:
    key = jax.random.key(seed)
    ks = jax.random.split(key, 24)
    f32 = jnp.float32

    def nrm(k, shape, scale):
        return jax.random.normal(k, shape, f32) * scale

    def gain(k, shape):
        return 1.0 + 0.02 * jax.random.normal(k, shape, f32)

    L, D = DEPTH, D_MODEL
    return {
        'x_prompt': nrm(ks[0], (BATCH, SEQ, D), 1.0),
        'x_sample': nrm(ks[1], (DEC_BATCH, DEC_SEQ, D), 1.0),
        'c_prompt': nrm(ks[2], (BATCH, D), 1.0),
        'c_sample': nrm(ks[3], (DEC_BATCH, D), 1.0),
        'w_ada': nrm(ks[4], (L, D, N_MOD * D), 0.5 * D ** -0.5),
        'b_ada': nrm(ks[5], (L, N_MOD * D), 0.02),
        'g_mix': gain(ks[6], (L, D)),
        'g_ffn': gain(ks[7], (L, D)),
        'w_in': nrm(ks[8], (L, D, IN_WIDTH), D ** -0.5),
        'g_q': gain(ks[9], (L, Q_LORA_RANK)),
        'w_uq': nrm(ks[10], (L, Q_LORA_RANK, N_HEADS * QK_DIM), Q_LORA_RANK ** -0.5),
        'g_kv': gain(ks[11], (L, KV_LORA_RANK)),
        'w_ukv': nrm(ks[12], (L, KV_LORA_RANK, N_HEADS * (QK_NOPE_DIM + V_HEAD_DIM)), KV_LORA_RANK ** -0.5),
        'w_a': nrm(ks[13], (L, ATTN_WIDTH, D), ATTN_WIDTH ** -0.5),
        'w_b': nrm(ks[14], (L, F_WIDTH, D), F_WIDTH ** -0.5),
        'w_out': nrm(ks[15], (L, D, D), D ** -0.5),
        'w_router': nrm(ks[16], (L, D, N_EXPERTS), D ** -0.5),
        'b_router': nrm(ks[17], (L, N_EXPERTS), 0.01),
        'w_gu': nrm(ks[18], (L, N_EXPERTS, D, 2 * D_FF), D ** -0.5),
        'b_gu': nrm(ks[19], (L, N_EXPERTS, 2 * D_FF), 0.02),
        'w_dn': nrm(ks[20], (L, N_EXPERTS, D_FF, D), D_FF ** -0.5),
        'b_dn': nrm(ks[21], (L, N_EXPERTS, D), 0.02),
        'g_final': gain(ks[22], (D,)),
    }


def reference(x_prompt, x_sample, c_prompt, c_sample, w_ada, b_ada, g_mix, g_ffn, w_in, g_q, w_uq,
              g_kv, w_ukv, w_a, w_b, w_out, w_router, b_router, w_gu, b_gu, w_dn, b_dn, g_final):
    y_prompt = trunk(x_prompt, c_prompt, w_ada, b_ada, g_mix, g_ffn, w_in, g_q, w_uq, g_kv, w_ukv,
                     w_a, w_b, w_out, w_router, b_router, w_gu, b_gu, w_dn, b_dn, g_final)
    y_sample = trunk(x_sample, c_sample, w_ada, b_ada, g_mix, g_ffn, w_in, g_q, w_uq, g_kv, w_ukv,
                     w_a, w_b, w_out, w_router, b_router, w_gu, b_gu, w_dn, b_dn, g_final)
    return (y_prompt, y_sample)
```

```python
import functools
import math

import numpy as np
import jax
import jax.numpy as jnp
from jax import lax
from jax.experimental import pallas as pl
from jax.experimental.pallas import tpu as pltpu

N_HEADS = 8
QK_NOPE_DIM = 64
QK_ROPE_DIM = 32
V_HEAD_DIM = 64
QK_DIM = QK_NOPE_DIM + QK_ROPE_DIM
Q_LORA_RANK = 384
KV_LORA_RANK = 256
F_GROUPS = 8
F_GROUP_DIM = 64
F_WIDTH = F_GROUPS * F_GROUP_DIM
N_EXPERTS = 32
TOP_K = 4
SWIGLU_LIMIT = 7.0
SWIGLU_ALPHA = 1.702
ROPE_THETA = 10000.0
RMS_EPS = 1e-6
N_MOD = 6

LANES = 128
SUBLANES = 8
VMEM_LIMIT_BYTES = 60 * 1024 * 1024

HEAD_PAD = LANES
KR_PAD = LANES
ROPE_HALF = QK_ROPE_DIM // 2
FFT_N2 = 64
MOE_ROWS = 128
NEG_BIG = -1e30

F32 = jnp.float32
BF16 = jnp.bfloat16


def _tiles(seq, n_tok):
    tm = min(512, seq)
    tq = min(512, seq)
    tk = min(512, seq)
    tc = min(4096, n_tok)
    return tm, tq, tk, tc


def _cparams(sem):
    return pltpu.CompilerParams(dimension_semantics=sem, vmem_limit_bytes=VMEM_LIMIT_BYTES)


def _dot(a, b):
    return jnp.dot(a, b, preferred_element_type=F32)


def _dot_nt(a, b):
    return lax.dot_general(a, b, (((1,), (1,)), ((), ())), preferred_element_type=F32)


def _dot_tn(a, b):
    return lax.dot_general(a, b, (((0,), (0,)), ((), ())), preferred_element_type=F32)


def _split_bf16(x):
    hi = x.astype(BF16)
    lo = (x - hi.astype(F32)).astype(BF16)
    return hi, lo


def _rms(x, eps=RMS_EPS):
    return x * lax.rsqrt(jnp.mean(x * x, axis=-1, keepdims=True) + eps)


def _mod_kernel(c_ref, w_ref, b_ref, o_ref):
    c = c_ref[...]
    sc = c * jax.nn.sigmoid(c)
    c_hi, c_lo = _split_bf16(sc)
    w_hi, w_lo = _split_bf16(w_ref[...])
    o_ref[...] = _dot(c_hi, w_hi) + _dot(c_hi, w_lo) + _dot(c_lo, w_hi) + b_ref[...]


def _adaln_mod(c_all, w_ada, b_ada):
    n_layers, d, nd = w_ada.shape
    rows = c_all.shape[0]
    tn = min(nd, 1536)
    return pl.pallas_call(
        _mod_kernel,
        grid=(n_layers, nd // tn),
        in_specs=[
            pl.BlockSpec((rows, d), lambda l, j: (0, 0)),
            pl.BlockSpec((None, d, tn), lambda l, j: (l, 0, j)),
            pl.BlockSpec((None, 1, tn), lambda l, j: (l, 0, j)),
        ],
        out_specs=pl.BlockSpec((None, rows, tn), lambda l, j: (l, 0, j)),
        out_shape=jax.ShapeDtypeStruct((n_layers, rows, nd), F32),
        compiler_params=_cparams(("arbitrary", "arbitrary")),
        name="adaln_mod",
    )(c_all, w_ada, b_ada.reshape(n_layers, 1, nd))


def _unpack_rows(blk_ref, n_rows):
    nc = blk_ref.shape[0] // n_rows
    blk = blk_ref[...].reshape(n_rows // SUBLANES, nc, SUBLANES, LANES)
    return jnp.concatenate([blk[:, c].reshape(n_rows, LANES) for c in range(nc)], axis=1)


def _pack_rows(x):
    n_rows, width = x.shape
    nc = width // LANES
    parts = [x[:, c * LANES:(c + 1) * LANES].reshape(n_rows // SUBLANES, SUBLANES, LANES) for c in range(nc)]
    return jnp.stack(parts, axis=1).reshape(n_rows * nc, LANES)


def _rope(x, ctab, s1tab, s2tab):
    width = x.shape[1]
    return x * ctab + pltpu.roll(x, width - ROPE_HALF, 1) * s1tab + pltpu.roll(x, ROPE_HALF, 1) * s2tab


def _in_proj_kernel(*refs, d, has_moe):
    if has_moe:
        (x_ref, moe_ref, modp_ref, mod_ref, gmix_ref, win_ref, gq_ref, wuq_ref, gkv_ref, wkn_ref, wv_ref,
         ctab_ref, s1_ref, s2_ref, bdc_ref, bds_ref,
         q_ref, k_ref, vt_ref, w_ref, gates_ref, x2_ref) = refs
    else:
        (x_ref, mod_ref, gmix_ref, win_ref, gq_ref, wuq_ref, gkv_ref, wkn_ref, wv_ref,
         ctab_ref, s1_ref, s2_ref, bdc_ref, bds_ref,
         q_ref, k_ref, vt_ref, w_ref, gates_ref) = refs
    tm = x_ref.shape[0]
    x = x_ref[...]
    if has_moe:
        x = x + modp_ref[5:6, :] * _unpack_rows(moe_ref, tm)
        x2_ref[...] = x
    h = (_rms(x) * gmix_ref[...]) * (1.0 + mod_ref[1:2, :]) + mod_ref[0:1, :]
    hb = h.astype(BF16)

    o1 = Q_LORA_RANK
    o2 = o1 + KV_LORA_RANK
    o3 = o2 + KR_PAD
    o4 = o3 + F_WIDTH
    o5 = o4 + d

    ctab = jnp.concatenate([ctab_ref[...]] * N_HEADS, axis=1)
    s1tab = jnp.concatenate([s1_ref[...]] * N_HEADS, axis=1)
    s2tab = jnp.concatenate([s2_ref[...]] * N_HEADS, axis=1)

    uq = _dot(hb, win_ref[:, 0:o1])
    rq = (_rms(uq) * gq_ref[...]).astype(BF16)
    q = _rope(_dot(rq, wuq_ref[...]), ctab, s1tab, s2tab) * (QK_DIM ** -0.5)
    for hd in range(N_HEADS):
        q_ref[hd, :, :] = q[:, hd * HEAD_PAD:(hd + 1) * HEAD_PAD].astype(BF16)

    ukv = _dot(hb, win_ref[:, o1:o2])
    rkv = (_rms(ukv) * gkv_ref[...]).astype(BF16)
    ukr = _dot(hb, win_ref[:, o2:o3])
    kfull = _dot(rkv, wkn_ref[...]) + jnp.concatenate([ukr] * N_HEADS, axis=1)
    kk = _rope(kfull, ctab, s1tab, s2tab)
    for hd in range(N_HEADS):
        k_ref[hd, :, :] = kk[:, hd * HEAD_PAD:(hd + 1) * HEAD_PAD].astype(BF16)

    v = _dot(rkv, wv_ref[...])
    vt_ref[...] = v.T.astype(BF16)

    ufb = _dot(hb, win_ref[:, o3:o4]).astype(BF16)
    w_ref[0, :, :] = _dot(ufb, bdc_ref[...]).astype(BF16)
    w_ref[1, :, :] = (-_dot(ufb, bds_ref[...])).astype(BF16)

    gates_ref[:, 0:d] = jax.nn.sigmoid(_dot(hb, win_ref[:, o4:o5])).astype(BF16)
    gates_ref[:, d:2 * d] = jax.nn.sigmoid(_dot(hb, win_ref[:, o5:o5 + d])).astype(BF16)


def _in_proj(x, moe_prev, mod_prev, mod, gmix, win, gq, wuq, gkv, wkn, wv, tabs, bdc, bds, *, bsz, seq):
    n_tok, d = x.shape
    tm, _, _, _ = _tiles(seq, n_tok)
    has_moe = moe_prev is not None
    per_b = seq // tm
    nch = d // LANES
    win_w = win.shape[1]

    def tok(i):
        return (i, 0)

    def modmap(i):
        return (i // per_b, 0, 0)

    def const2(i):
        return (0, 0)

    def tabmap(i):
        return (i % per_b, 0)

    in_specs = [pl.BlockSpec((tm, d), tok)]
    args = [x]
    if has_moe:
        in_specs += [pl.BlockSpec((tm * nch, LANES), tok),
                     pl.BlockSpec((None, N_MOD, d), modmap)]
        args += [moe_prev, mod_prev]
    in_specs += [
        pl.BlockSpec((None, N_MOD, d), modmap),
        pl.BlockSpec((1, d), const2),
        pl.BlockSpec((d, win_w), const2),
        pl.BlockSpec((1, Q_LORA_RANK), const2),
        pl.BlockSpec((Q_LORA_RANK, N_HEADS * HEAD_PAD), const2),
        pl.BlockSpec((1, KV_LORA_RANK), const2),
        pl.BlockSpec((KV_LORA_RANK, N_HEADS * HEAD_PAD), const2),
        pl.BlockSpec((KV_LORA_RANK, N_HEADS * V_HEAD_DIM), const2),
        pl.BlockSpec((tm, HEAD_PAD), tabmap),
        pl.BlockSpec((tm, HEAD_PAD), tabmap),
        pl.BlockSpec((tm, HEAD_PAD), tabmap),
        pl.BlockSpec((F_WIDTH, F_WIDTH), const2),
        pl.BlockSpec((F_WIDTH, F_WIDTH), const2),
    ]
    args += [mod, gmix, win, gq, wuq, gkv, wkn, wv, tabs[0], tabs[1], tabs[2], bdc, bds]

    hv = N_HEADS * V_HEAD_DIM
    out_specs = [
        pl.BlockSpec((N_HEADS, tm, HEAD_PAD), lambda i: (0, i, 0)),
        pl.BlockSpec((N_HEADS, tm, HEAD_PAD), lambda i: (0, i, 0)),
        pl.BlockSpec((None, hv, tm), lambda i: (i, 0, 0)),
        pl.BlockSpec((2, tm, F_WIDTH), lambda i: (0, i, 0)),
        pl.BlockSpec((tm, 2 * d), tok),
    ]
    out_shape = [
        jax.ShapeDtypeStruct((N_HEADS, n_tok, HEAD_PAD), BF16),
        jax.ShapeDtypeStruct((N_HEADS, n_tok, HEAD_PAD), BF16),
        jax.ShapeDtypeStruct((n_tok // tm, hv, tm), BF16),
        jax.ShapeDtypeStruct((2, n_tok, F_WIDTH), BF16),
        jax.ShapeDtypeStruct((n_tok, 2 * d), BF16),
    ]
    if has_moe:
        out_specs.append(pl.BlockSpec((tm, d), tok))
        out_shape.append(jax.ShapeDtypeStruct((n_tok, d), F32))
    outs = pl.pallas_call(
        functools.partial(_in_proj_kernel, d=d, has_moe=has_moe),
        grid=(n_tok // tm,),
        in_specs=in_specs,
        out_specs=out_specs,
        out_shape=out_shape,
        compiler_params=_cparams(("arbitrary",)),
        name="in_proj",
    )(*args)
    if has_moe:
        return outs
    return (*outs, x)


def _attn_kernel(q_ref, k_ref, vt_ref, o_ref, *, tk, n_kv):
    q = q_ref[...]
    tq = q.shape[0]

    def body(j, carry):
        m, l, acc = carry
        kj = k_ref[pl.ds(pl.multiple_of(j * tk, tk), tk), :]
        s = _dot_nt(kj, q)
        m_new = jnp.maximum(m, jnp.max(s, axis=0, keepdims=True))
        p = jnp.exp(s - m_new)
        alpha = jnp.exp(m - m_new)
        l_new = alpha * l + jnp.sum(p, axis=0, keepdims=True)
        acc_new = alpha * acc + _dot(vt_ref[j], p.astype(BF16))
        return m_new, l_new, acc_new

    init = (jnp.full((1, tq), NEG_BIG, F32), jnp.zeros((1, tq), F32), jnp.zeros((V_HEAD_DIM, tq), F32))
    _, l, acc = lax.fori_loop(0, n_kv, body, init)
    o_ref[...] = (acc / l).astype(BF16)


def _attention(q, k, vt, *, bsz, seq):
    n_tok = q.shape[1]
    _, tq, tk, _ = _tiles(seq, n_tok)
    n_q = seq // tq
    n_kv = seq // tk
    return pl.pallas_call(
        functools.partial(_attn_kernel, tk=tk, n_kv=n_kv),
        grid=(bsz, N_HEADS, n_q),
        in_specs=[
            pl.BlockSpec((None, tq, HEAD_PAD), lambda b, h, i: (h, b * n_q + i, 0)),
            pl.BlockSpec((None, seq, HEAD_PAD), lambda b, h, i: (h, b, 0)),
            pl.BlockSpec((n_kv, V_HEAD_DIM, tk), lambda b, h, i: (b, h, 0)),
        ],
        out_specs=pl.BlockSpec((V_HEAD_DIM, tq), lambda b, h, i: (h, b * n_q + i)),
        out_shape=jax.ShapeDtypeStruct((N_HEADS * V_HEAD_DIM, n_tok), BF16),
        compiler_params=_cparams(("arbitrary", "arbitrary", "arbitrary")),
        name="attention",
    )(q, k, vt)


def _fft1_kernel(x_ref, f_ref, tr_ref, ti_ref, o_ref):
    n1 = tr_ref.shape[0]
    a = _dot(f_ref[...], x_ref[...])
    ar, ai = a[:n1], a[n1:]
    tr, ti = tr_ref[...], ti_ref[...]
    o_ref[0, :, :] = (ar * tr - ai * ti).astype(BF16)
    o_ref[1, :, :] = (ar * ti + ai * tr).astype(BF16)


def _fft2_kernel(x_ref, g_ref, o_ref):
    o_ref[...] = _dot(g_ref[...], x_ref[...]).astype(BF16)


def _seq_dft_real(w, consts, *, bsz, seq):
    f1, tr, ti, g2 = consts
    n2 = FFT_N2
    n1 = seq // n2
    fw = w.shape[2]
    lanes = n2 * fw
    lt = min(4096, lanes)
    xw = w.reshape(2, bsz, n1, lanes).transpose(1, 0, 2, 3).reshape(bsz, 2 * n1, lanes)
    b1 = pl.pallas_call(
        _fft1_kernel,
        grid=(bsz, lanes // lt),
        in_specs=[
            pl.BlockSpec((None, 2 * n1, lt), lambda b, j: (b, 0, j)),
            pl.BlockSpec((2 * n1, 2 * n1), lambda b, j: (0, 0)),
            pl.BlockSpec((n1, lt), lambda b, j: (0, j)),
            pl.BlockSpec((n1, lt), lambda b, j: (0, j)),
        ],
        out_specs=pl.BlockSpec((None, 2, n1, lt), lambda b, j: (b, 0, 0, j)),
        out_shape=jax.ShapeDtypeStruct((bsz, 2, n1, lanes), BF16),
        compiler_params=_cparams(("arbitrary", "arbitrary")),
        name="fft_stage1",
    )(xw, f1, tr, ti)
    lanes2 = n1 * fw
    lt2 = min(4096, lanes2)
    bt = b1.reshape(bsz, 2, n1, n2, fw).transpose(0, 1, 3, 2, 4).reshape(bsz, 2 * n2, lanes2)
    y = pl.pallas_call(
        _fft2_kernel,
        grid=(bsz, lanes2 // lt2),
        in_specs=[
            pl.BlockSpec((None, 2 * n2, lt2), lambda b, j: (b, 0, j)),
            pl.BlockSpec((n2, 2 * n2), lambda b, j: (0, 0)),
        ],
        out_specs=pl.BlockSpec((None, n2, lt2), lambda b, j: (b, 0, j)),
        out_shape=jax.ShapeDtypeStruct((bsz, n2, lanes2), BF16),
        compiler_params=_cparams(("arbitrary", "arbitrary")),
        name="fft_stage2",
    )(bt, g2)
    return y.reshape(bsz * seq, fw)


def _dft_consts(seq):
    n2 = FFT_N2
    n1 = seq // n2
    k1 = np.arange(n1)
    ang1 = 2.0 * np.pi * np.outer(k1, k1) / n1
    c1, s1 = np.cos(ang1), np.sin(ang1)
    f1 = np.block([[c1, s1], [-s1, c1]])
    k2 = np.arange(n2)
    ang2 = 2.0 * np.pi * np.outer(k2, k2) / n2
    scale = 1.0 / math.sqrt(seq * F_GROUP_DIM)
    g2 = np.concatenate([np.cos(ang2), np.sin(ang2)], axis=1) * scale
    prod = (jnp.arange(n1, dtype=jnp.int32)[:, None] * jnp.arange(n2, dtype=jnp.int32)[None, :]) % seq
    ang = prod.astype(F32) * (2.0 * math.pi / seq)
    tr = jnp.repeat(jnp.cos(ang), F_WIDTH, axis=1)
    ti = jnp.repeat(-jnp.sin(ang), F_WIDTH, axis=1)
    return jnp.asarray(f1, BF16), tr, ti, jnp.asarray(g2, BF16)


def _channel_dft_consts():
    j = np.arange(F_GROUP_DIM)
    ang = 2.0 * np.pi * np.outer(j, j) / F_GROUP_DIM
    eye = np.eye(F_GROUPS)
    return jnp.asarray(np.kron(eye, np.cos(ang)), BF16), jnp.asarray(np.kron(eye, np.sin(ang)), BF16)


def _rope_tabs(seq):
    inv = 1.0 / (ROPE_THETA ** (jnp.arange(0, QK_ROPE_DIM, 2, dtype=F32) / QK_ROPE_DIM))
    ang = jnp.arange(seq, dtype=F32)[:, None] * inv[None, :]
    cos, sin = jnp.cos(ang), jnp.sin(ang)
    ones = jnp.ones((seq, QK_NOPE_DIM), F32)
    zeros = jnp.zeros((seq, QK_NOPE_DIM), F32)
    pad1 = jnp.ones((seq, HEAD_PAD - QK_DIM), F32)
    pad0 = jnp.zeros((seq, HEAD_PAD - QK_DIM), F32)
    zr = jnp.zeros_like(sin)
    ctab = jnp.concatenate([ones, cos, cos, pad1], axis=1)
    s1tab = jnp.concatenate([zeros, -sin, zr, pad0], axis=1)
    s2tab = jnp.concatenate([zeros, zr, sin, pad0], axis=1)
    return ctab, s1tab, s2tab


def _post_kernel(ot_ref, yf_ref, gates_ref, x_ref, mod_ref, wa_ref, wb_ref, wout_ref, gffn_ref,
                 wrh_ref, wrl_ref, br_ref, x1_ref, h2p_ref, ti_ref, tg_ref, *, d):
    tm = x_ref.shape[0]
    ya = _dot_tn(ot_ref[...], wa_ref[...])
    yb = _dot(yf_ref[...], wb_ref[...])
    merged = gates_ref[:, 0:d].astype(F32) * ya + gates_ref[:, d:2 * d].astype(F32) * yb
    x1 = x_ref[...] + mod_ref[2:3, :] * _dot(merged.astype(BF16), wout_ref[...])
    x1_ref[...] = x1
    h2 = (_rms(x1) * gffn_ref[...]) * (1.0 + mod_ref[4:5, :]) + mod_ref[3:4, :]

    h_hi, h_lo = _split_bf16(h2)
    logits = _dot(h_hi, wrh_ref[...]) + _dot(h_hi, wrl_ref[...]) + _dot(h_lo, wrh_ref[...]) + br_ref[...]
    lane = lax.broadcasted_iota(jnp.int32, logits.shape, 1)
    work = logits
    vals, idxs = [], []
    for _ in range(TOP_K):
        mx = jnp.max(work, axis=1, keepdims=True)
        ix = jnp.min(jnp.where(work == mx, lane, LANES), axis=1, keepdims=True)
        vals.append(mx)
        idxs.append(ix)
        work = jnp.where(lane == ix, -jnp.inf, work)
    exps = [jnp.exp(v - vals[0]) for v in vals]
    tot = exps[0]
    for e in exps[1:]:
        tot = tot + e
    ti = jnp.zeros(logits.shape, jnp.int32)
    tg = jnp.zeros(logits.shape, F32)
    for kk in range(TOP_K):
        ti = jnp.where(lane == kk, idxs[kk], ti)
        tg = jnp.where(lane == kk, exps[kk] / tot, tg)
    ti_ref[...] = ti
    tg_ref[...] = tg

    bits = lax.bitcast_convert_type(h2.astype(BF16).astype(F32), jnp.uint32)
    half = d // 2
    h2p_ref[...] = _pack_rows((bits[:, 0:half] >> 16) | (bits[:, half:d] & jnp.uint32(0xFFFF0000)))


def _post_mix(ot, yf, gates, x, mod, wa, wb, wout, gffn, wrh, wrl, br, *, bsz, seq):
    n_tok, d = x.shape
    tm, _, _, _ = _tiles(seq, n_tok)
    per_b = seq // tm
    hv = N_HEADS * V_HEAD_DIM
    nchp = d // 2 // LANES

    def tok(i):
        return (i, 0)

    def const2(i):
        return (0, 0)

    return pl.pallas_call(
        functools.partial(_post_kernel, d=d),
        grid=(n_tok // tm,),
        in_specs=[
            pl.BlockSpec((hv, tm), lambda i: (0, i)),
            pl.BlockSpec((tm, F_WIDTH), tok),
            pl.BlockSpec((tm, 2 * d), tok),
            pl.BlockSpec((tm, d), tok),
            pl.BlockSpec((None, N_MOD, d), lambda i: (i // per_b, 0, 0)),
            pl.BlockSpec((hv, d), const2),
            pl.BlockSpec((F_WIDTH, d), const2),
            pl.BlockSpec((d, d), const2),
            pl.BlockSpec((1, d), const2),
            pl.BlockSpec((d, LANES), const2),
            pl.BlockSpec((d, LANES), const2),
            pl.BlockSpec((1, LANES), const2),
        ],
        out_specs=[
            pl.BlockSpec((tm, d), tok),
            pl.BlockSpec((tm * nchp, LANES), tok),
            pl.BlockSpec((tm, LANES), tok),
            pl.BlockSpec((tm, LANES), tok),
        ],
        out_shape=[
            jax.ShapeDtypeStruct((n_tok, d), F32),
            jax.ShapeDtypeStruct((n_tok * nchp, LANES), jnp.uint32),
            jax.ShapeDtypeStruct((n_tok, LANES), jnp.int32),
            jax.ShapeDtypeStruct((n_tok, LANES), F32),
        ],
        compiler_params=_cparams(("arbitrary",)),
        name="post_mix",
    )(ot, yf, gates, x, mod, wa, wb, wout, gffn, wrh, wrl, br)


def _moe_kernel(bexp_ref, bval_ref, tok_ref, gate_ref, h2p_ref, wg_ref, wu_ref, wd_ref, bg_ref, bu_ref, bd_ref,
                out_ref, xs_ref, ys_ref, *, n_blk, tc):
    c = pl.program_id(0)
    j = pl.program_id(1)
    nchp = h2p_ref.shape[0] // tc
    nc = out_ref.shape[0] // tc
    rows = MOE_ROWS

    @pl.when(j == 0)
    def _():
        out_ref[...] = jnp.zeros_like(out_ref)

    @pl.when(bval_ref[c * n_blk + j] > 0)
    def _():
        src2, out2, ys2 = h2p_ref, out_ref, ys_ref
        gp = nchp * SUBLANES
        go = nc * SUBLANES
        for m in range(rows):
            t = tok_ref[0, 0, m]
            xs_ref[pl.ds((m >> 3) * gp + (m & 7), nchp, stride=SUBLANES), :] = (
                src2[pl.ds((t >> 3) * gp + (t & 7), nchp, stride=SUBLANES), :])
        xs4 = xs_ref[...].reshape(rows // SUBLANES, nchp, SUBLANES, LANES)
        los, his = [], []
        for cc in range(nchp):
            wd = xs4[:, cc].reshape(rows, LANES)
            los.append(lax.bitcast_convert_type(wd << 16, F32))
            his.append(lax.bitcast_convert_type(wd & jnp.uint32(0xFFFF0000), F32))
        x = jnp.concatenate(los + his, axis=1).astype(BF16)
        g = jnp.minimum(_dot(x, wg_ref[...]) + bg_ref[...], SWIGLU_LIMIT)
        u = jnp.clip(_dot(x, wu_ref[...]) + bu_ref[...], -SWIGLU_LIMIT, SWIGLU_LIMIT)
        act = ((u + 1.0) * (g * jax.nn.sigmoid(g * SWIGLU_ALPHA))).astype(BF16)
        y = _dot(act, wd_ref[...]) + bd_ref[...]
        ys_ref[...] = _pack_rows(y)
        batch = 4
        for m0 in range(0, rows, batch):
            pend = []
            for m in range(m0, m0 + batch):
                t = tok_ref[0, 0, m]
                base = (t >> 3) * go + (t & 7)
                rv = ys2[pl.ds((m >> 3) * go + (m & 7), nc, stride=SUBLANES), :]
                pend.append((base, out2[pl.ds(base, nc, stride=SUBLANES), :] + rv * gate_ref[0, 0, m]))
            for base, val in pend:
                out2[pl.ds(base, nc, stride=SUBLANES), :] = val


def _moe_ffn(h2p, top_i, top_g, wg, wu, wd, bg, bu, bd, *, d):
    n_tok = top_i.shape[0]
    nchp = h2p.shape[0] // n_tok
    _, _, _, tc = _tiles(n_tok, n_tok)
    nc = d // LANES
    rows = MOE_ROWS
    n_chunks = n_tok // tc
    n_asg = tc * TOP_K
    n_blk = n_asg // rows + N_EXPERTS
    n_exp, _, ff = wg.shape

    e_flat = top_i.reshape(n_chunks, n_asg)
    g_flat = top_g.reshape(n_chunks, n_asg)
    key = e_flat * n_asg + jnp.arange(n_asg, dtype=jnp.int32)[None, :]
    key = jnp.sort(key, axis=1)
    order = key % n_asg
    counts = jnp.sum((e_flat[:, :, None] == jnp.arange(n_exp, dtype=jnp.int32)[None, None, :]).astype(jnp.int32),
                     axis=1)
    start = jnp.cumsum(counts, axis=1) - counts
    nb_e = (counts + rows - 1) // rows
    blk_end = jnp.cumsum(nb_e, axis=1)
    blk_start = blk_end - nb_e
    jb = jnp.arange(n_blk, dtype=jnp.int32)
    e_of_blk = jnp.sum((jb[None, :, None] >= blk_end[:, None, :]).astype(jnp.int32), axis=2)
    blk_valid = e_of_blk < n_exp
    last_e = jnp.max(jnp.where(counts > 0, jnp.arange(n_exp, dtype=jnp.int32)[None, :], 0), axis=1, keepdims=True)
    e_blk = jnp.where(blk_valid, jnp.minimum(e_of_blk, n_exp - 1), last_e)
    q0 = (jb[None, :] - jnp.take_along_axis(blk_start, e_blk, axis=1)) * rows
    cnt_blk = jnp.take_along_axis(counts, e_blk, axis=1)
    n_valid = jnp.where(blk_valid, jnp.clip(cnt_blk - q0, 0, rows), 0)
    r = jnp.arange(rows, dtype=jnp.int32)
    q = q0[:, :, None] + r[None, None, :]
    row_ok = r[None, None, :] < n_valid[:, :, None]
    s_idx = jnp.clip(jnp.take_along_axis(start, e_blk, axis=1)[:, :, None] + q, 0, n_asg - 1)
    a_idx = jnp.take_along_axis(order, s_idx.reshape(n_chunks, n_blk * rows), axis=1)
    row_tok = jnp.where(row_ok.reshape(n_chunks, -1), a_idx // TOP_K, 0).astype(jnp.int32)
    row_gate = jnp.where(row_ok.reshape(n_chunks, -1), jnp.take_along_axis(g_flat, a_idx, axis=1), 0.0)
    row_tok = row_tok.reshape(n_chunks * n_blk, 1, rows)
    row_gate = row_gate.reshape(n_chunks * n_blk, 1, rows).astype(F32)
    bexp = e_blk.reshape(-1).astype(jnp.int32)
    bval = n_valid.reshape(-1).astype(jnp.int32)

    def wmap(c, j, bexp_ref, bval_ref):
        return (bexp_ref[c * n_blk + j], 0, 0)

    def rowmap(c, j, bexp_ref, bval_ref):
        return (c * n_blk + j, 0, 0)

    def chunkmap(c, j, bexp_ref, bval_ref):
        return (c, 0)

    grid_spec = pltpu.PrefetchScalarGridSpec(
        num_scalar_prefetch=2,
        grid=(n_chunks, n_blk),
        in_specs=[
            pl.BlockSpec((1, 1, rows), rowmap, memory_space=pltpu.SMEM),
            pl.BlockSpec((1, 1, rows), rowmap, memory_space=pltpu.SMEM),
            pl.BlockSpec((tc * nchp, LANES), chunkmap, pipeline_mode=pl.Buffered(1)),
            pl.BlockSpec((None, d, ff), wmap),
            pl.BlockSpec((None, d, ff), wmap),
            pl.BlockSpec((None, ff, d), wmap),
            pl.BlockSpec((None, 1, ff), wmap),
            pl.BlockSpec((None, 1, ff), wmap),
            pl.BlockSpec((None, 1, d), wmap),
        ],
        out_specs=pl.BlockSpec((tc * nc, LANES), chunkmap),
        scratch_shapes=[
            pltpu.VMEM((rows * nchp, LANES), jnp.uint32),
            pltpu.VMEM((rows * nc, LANES), F32),
        ],
    )
    return pl.pallas_call(
        functools.partial(_moe_kernel, n_blk=n_blk, tc=tc),
        grid_spec=grid_spec,
        out_shape=jax.ShapeDtypeStruct((n_tok * nc, LANES), F32),
        compiler_params=_cparams(("arbitrary", "arbitrary")),
        name="moe_ffn",
    )(bexp, bval, row_tok, row_gate, h2p, wg, wu, wd, bg, bu, bd)


def _final_kernel(x_ref, moe_ref, mod_ref, g_ref, o_ref):
    tm = x_ref.shape[0]
    x = x_ref[...] + mod_ref[5:6, :] * _unpack_rows(moe_ref, tm)
    o_ref[...] = _rms(x) * g_ref[...]


def _final_norm(x, moe, mod, g_final, *, bsz, seq):
    n_tok, d = x.shape
    tm, _, _, _ = _tiles(seq, n_tok)
    per_b = seq // tm
    nch = d // LANES
    return pl.pallas_call(
        _final_kernel,
        grid=(n_tok // tm,),
        in_specs=[
            pl.BlockSpec((tm, d), lambda i: (i, 0)),
            pl.BlockSpec((tm * nch, LANES), lambda i: (i, 0)),
            pl.BlockSpec((None, N_MOD, d), lambda i: (i // per_b, 0, 0)),
            pl.BlockSpec((1, d), lambda i: (0, 0)),
        ],
        out_specs=pl.BlockSpec((tm, d), lambda i: (i, 0)),
        out_shape=jax.ShapeDtypeStruct((n_tok, d), F32),
        compiler_params=_cparams(("arbitrary",)),
        name="final_norm",
    )(x, moe, mod, g_final)


def _prep_weights(w_in, w_uq, w_ukv, w_a, w_b, w_out, w_router, b_router, w_gu, b_gu, w_dn, b_dn):
    n_layers, d, _ = w_in.shape
    o1 = Q_LORA_RANK
    o2 = o1 + KV_LORA_RANK
    o3 = o2 + QK_ROPE_DIM
    kr = jnp.zeros((n_layers, d, KR_PAD), w_in.dtype).at[:, :, QK_NOPE_DIM:QK_DIM].set(w_in[:, :, o2:o3])
    win = jnp.concatenate([w_in[:, :, :o2], kr, w_in[:, :, o3:]], axis=2).astype(BF16)
    wuq = jnp.pad(w_uq.reshape(n_layers, Q_LORA_RANK, N_HEADS, QK_DIM),
                  ((0, 0), (0, 0), (0, 0), (0, HEAD_PAD - QK_DIM)))
    wuq = wuq.reshape(n_layers, Q_LORA_RANK, N_HEADS * HEAD_PAD).astype(BF16)
    wkv = w_ukv.reshape(n_layers, KV_LORA_RANK, N_HEADS, QK_NOPE_DIM + V_HEAD_DIM)
    wkn = jnp.pad(wkv[..., :QK_NOPE_DIM], ((0, 0), (0, 0), (0, 0), (0, HEAD_PAD - QK_NOPE_DIM)))
    wkn = wkn.reshape(n_layers, KV_LORA_RANK, N_HEADS * HEAD_PAD).astype(BF16)
    wv = wkv[..., QK_NOPE_DIM:].reshape(n_layers, KV_LORA_RANK, N_HEADS * V_HEAD_DIM).astype(BF16)
    n_exp = w_router.shape[2]
    wr = jnp.pad(w_router, ((0, 0), (0, 0), (0, LANES - n_exp)))
    wr_hi = wr.astype(BF16)
    wr_lo = (wr - wr_hi.astype(F32)).astype(BF16)
    br = jnp.pad(b_router, ((0, 0), (0, LANES - n_exp)), constant_values=NEG_BIG)[:, None, :]
    wg = w_gu[..., 0::2].astype(BF16)
    wu = w_gu[..., 1::2].astype(BF16)
    bg = b_gu[..., 0::2][:, :, None, :]
    bu = b_gu[..., 1::2][:, :, None, :]
    wd = w_dn.astype(BF16)
    bd = b_dn[:, :, None, :]
    return dict(win=win, wuq=wuq, wkn=wkn, wv=wv, wa=w_a.astype(BF16), wb=w_b.astype(BF16),
                wout=w_out.astype(BF16), wr_hi=wr_hi, wr_lo=wr_lo, br=br, wg=wg, wu=wu, bg=bg, bu=bu, wd=wd, bd=bd)


def _trunk(x3, mod, wts, g_mix, g_ffn, g_q, g_kv, g_final, bdc, bds):
    bsz, seq, d = x3.shape
    n_layers = g_mix.shape[0]
    x = x3.reshape(bsz * seq, d)
    tabs = _rope_tabs(seq)
    dft = _dft_consts(seq)
    moe = None
    for l in range(n_layers):
        mod_prev = mod[l - 1] if l > 0 else None
        q, k, vt, w, gates, x = _in_proj(
            x, moe, mod_prev, mod[l], g_mix[l][None, :], wts["win"][l], g_q[l][None, :], wts["wuq"][l],
            g_kv[l][None, :], wts["wkn"][l], wts["wv"][l], tabs, bdc, bds, bsz=bsz, seq=seq)
        ot = _attention(q, k, vt, bsz=bsz, seq=seq)
        yf = _seq_dft_real(w, dft, bsz=bsz, seq=seq)
        x, h2p, top_i, top_g = _post_mix(
            ot, yf, gates, x, mod[l], wts["wa"][l], wts["wb"][l], wts["wout"][l], g_ffn[l][None, :],
            wts["wr_hi"][l], wts["wr_lo"][l], wts["br"][l], bsz=bsz, seq=seq)
        moe = _moe_ffn(h2p, top_i[:, :TOP_K], top_g[:, :TOP_K], wts["wg"][l], wts["wu"][l], wts["wd"][l],
                       wts["bg"][l], wts["bu"][l], wts["bd"][l], d=d)
    y = _final_norm(x, moe, mod[n_layers - 1], g_final[None, :], bsz=bsz, seq=seq)
    return y.reshape(bsz, seq, d)


def kernel(x_prompt, x_sample, c_prompt, c_sample, w_ada, b_ada, g_mix, g_ffn, w_in, g_q, w_uq, g_kv, w_ukv, w_a, w_b, w_out, w_router, b_router, w_gu, b_gu, w_dn, b_dn, g_final):
    n_layers, d, _ = w_ada.shape
    bp, bs = c_prompt.shape[0], c_sample.shape[0]
    rows = -(-(bp + bs) // SUBLANES) * SUBLANES
    c_all = jnp.concatenate([c_prompt, c_sample, jnp.zeros((rows - bp - bs, d), c_prompt.dtype)], axis=0)
    mod = _adaln_mod(c_all, w_ada, b_ada).reshape(n_layers, rows, N_MOD, d)
    wts = _prep_weights(w_in, w_uq, w_ukv, w_a, w_b, w_out, w_router, b_router, w_gu, b_gu, w_dn, b_dn)
    bdc, bds = _channel_dft_consts()
    y_prompt = _trunk(x_prompt, mod[:, :bp], wts, g_mix, g_ffn, g_q, g_kv, g_final, bdc, bds)
    y_sample = _trunk(x_sample, mod[:, bp:bp + bs], wts, g_mix, g_ffn, g_q, g_kv, g_final, bdc, bds)
    return (y_prompt, y_sample)
```

```python
import functools
import math

import numpy as np
import jax
import jax.numpy as jnp
from jax import lax
from jax.experimental import pallas as pl
from jax.experimental.pallas import tpu as pltpu

N_HEADS = 8
QK_NOPE_DIM = 64
QK_ROPE_DIM = 32
V_HEAD_DIM = 64
QK_DIM = QK_NOPE_DIM + QK_ROPE_DIM
Q_LORA_RANK = 384
KV_LORA_RANK = 256
F_GROUPS = 8
F_GROUP_DIM = 64
F_WIDTH = F_GROUPS * F_GROUP_DIM
N_EXPERTS = 32
TOP_K = 4
SWIGLU_LIMIT = 7.0
SWIGLU_ALPHA = 1.702
ROPE_THETA = 10000.0
RMS_EPS = 1e-6
N_MOD = 6

LANES = 128
SUBLANES = 8
VMEM_LIMIT_BYTES = 60 * 1024 * 1024

HEAD_PAD = LANES
KR_PAD = LANES
ROPE_HALF = QK_ROPE_DIM // 2
FFT_N2 = 64
MOE_ROWS = 128
ATTN_SUM_ROWS = 16
LOG2_E = math.log2(math.e)
NEG_BIG = -1e30

F32 = jnp.float32
BF16 = jnp.bfloat16


def _tiles(seq, n_tok):
    tk = min(512, seq // 2)
    tm = tk
    tq = min(512, seq)
    tc = min(4096, n_tok)
    return tm, tq, tk, tc


def _cparams(sem, flags=None):
    return pltpu.CompilerParams(dimension_semantics=sem, vmem_limit_bytes=VMEM_LIMIT_BYTES, flags=flags)


def _dot(a, b):
    return jnp.dot(a, b, preferred_element_type=F32)


def _dot_nt(a, b):
    return lax.dot_general(a, b, (((1,), (1,)), ((), ())), preferred_element_type=F32)


def _dot_tn(a, b):
    return lax.dot_general(a, b, (((0,), (0,)), ((), ())), preferred_element_type=F32)


def _split_bf16(x):
    hi = x.astype(BF16)
    lo = (x - hi.astype(F32)).astype(BF16)
    return hi, lo


def _rms(x, eps=RMS_EPS):
    return x * lax.rsqrt(jnp.mean(x * x, axis=-1, keepdims=True) + eps)


def _mod_kernel(c_ref, w_ref, b_ref, o_ref):
    c = c_ref[...]
    sc = c * jax.nn.sigmoid(c)
    c_hi, c_lo = _split_bf16(sc)
    w_hi, w_lo = _split_bf16(w_ref[...])
    o_ref[...] = _dot(c_hi, w_hi) + _dot(c_hi, w_lo) + _dot(c_lo, w_hi) + b_ref[...]


def _adaln_mod(c_all, w_ada, b_ada):
    n_layers, d, nd = w_ada.shape
    rows = c_all.shape[0]
    tn = min(nd, 1536)
    return pl.pallas_call(
        _mod_kernel,
        grid=(n_layers, nd // tn),
        in_specs=[
            pl.BlockSpec((rows, d), lambda l, j: (0, 0)),
            pl.BlockSpec((None, d, tn), lambda l, j: (l, 0, j)),
            pl.BlockSpec((None, 1, tn), lambda l, j: (l, 0, j)),
        ],
        out_specs=pl.BlockSpec((None, rows, tn), lambda l, j: (l, 0, j)),
        out_shape=jax.ShapeDtypeStruct((n_layers, rows, nd), F32),
        compiler_params=_cparams(("arbitrary", "arbitrary")),
        name="adaln_mod",
    )(c_all, w_ada, b_ada.reshape(n_layers, 1, nd))


def _unpack_rows(blk_ref, n_rows):
    nc = blk_ref.shape[0] // n_rows
    blk = blk_ref[...].reshape(n_rows // SUBLANES, nc, SUBLANES, LANES)
    return jnp.concatenate([blk[:, c].reshape(n_rows, LANES) for c in range(nc)], axis=1)


def _pack_rows(x):
    n_rows, width = x.shape
    nc = width // LANES
    parts = [x[:, c * LANES:(c + 1) * LANES].reshape(n_rows // SUBLANES, SUBLANES, LANES) for c in range(nc)]
    return jnp.stack(parts, axis=1).reshape(n_rows * nc, LANES)


def _rope(x, ctab, s1tab, s2tab):
    width = x.shape[1]
    return x * ctab + pltpu.roll(x, width - ROPE_HALF, 1) * s1tab + pltpu.roll(x, ROPE_HALF, 1) * s2tab


def _in_proj_kernel(*refs, d, has_moe):
    if has_moe:
        (x_ref, moe_ref, modp_ref, mod_ref, gmix_ref, win_ref, gq_ref, wuq_ref, gkv_ref, wkn_ref, wv_ref,
         ctab_ref, s1_ref, s2_ref, bdc_ref, bds_ref,
         q_ref, k_ref, vt_ref, w_ref, gates_ref, x2_ref) = refs
    else:
        (x_ref, mod_ref, gmix_ref, win_ref, gq_ref, wuq_ref, gkv_ref, wkn_ref, wv_ref,
         ctab_ref, s1_ref, s2_ref, bdc_ref, bds_ref,
         q_ref, k_ref, vt_ref, w_ref, gates_ref) = refs
    tm = x_ref.shape[0]
    x = x_ref[...]
    if has_moe:
        x = x + modp_ref[5:6, :] * _unpack_rows(moe_ref, tm)
        x2_ref[...] = x
    h = (_rms(x) * gmix_ref[...]) * (1.0 + mod_ref[1:2, :]) + mod_ref[0:1, :]
    hb = h.astype(BF16)

    o1 = Q_LORA_RANK
    o2 = o1 + KV_LORA_RANK
    o3 = o2 + KR_PAD
    o4 = o3 + F_WIDTH
    o5 = o4 + d

    ctab = jnp.concatenate([ctab_ref[...]] * N_HEADS, axis=1)
    s1tab = jnp.concatenate([s1_ref[...]] * N_HEADS, axis=1)
    s2tab = jnp.concatenate([s2_ref[...]] * N_HEADS, axis=1)

    uq = _dot(hb, win_ref[:, 0:o1])
    rq = (_rms(uq) * gq_ref[...]).astype(BF16)
    q = _rope(_dot(rq, wuq_ref[...]), ctab, s1tab, s2tab) * (QK_DIM ** -0.5 * LOG2_E)
    for hd in range(N_HEADS):
        q_ref[hd, :, :] = q[:, hd * HEAD_PAD:(hd + 1) * HEAD_PAD].astype(BF16)

    ukv = _dot(hb, win_ref[:, o1:o2])
    rkv = (_rms(ukv) * gkv_ref[...]).astype(BF16)
    ukr = _dot(hb, win_ref[:, o2:o3])
    kfull = _dot(rkv, wkn_ref[...]) + jnp.concatenate([ukr] * N_HEADS, axis=1)
    kk = _rope(kfull, ctab, s1tab, s2tab)
    for hd in range(N_HEADS):
        k_ref[hd, :, :] = kk[:, hd * HEAD_PAD:(hd + 1) * HEAD_PAD].astype(BF16)

    v = _dot(rkv, wv_ref[...])
    vt_ref[...] = v.T.astype(BF16)

    ufb = _dot(hb, win_ref[:, o3:o4]).astype(BF16)
    w_ref[0, :, :] = _dot(ufb, bdc_ref[...]).astype(BF16)
    w_ref[1, :, :] = (-_dot(ufb, bds_ref[...])).astype(BF16)

    gates_ref[:, 0:d] = jax.nn.sigmoid(_dot(hb, win_ref[:, o4:o5])).astype(BF16)
    gates_ref[:, d:2 * d] = jax.nn.sigmoid(_dot(hb, win_ref[:, o5:o5 + d])).astype(BF16)


def _in_proj(x, moe_prev, mod_prev, mod, gmix, win, gq, wuq, gkv, wkn, wv, tabs, bdc, bds, *, bsz, seq):
    n_tok, d = x.shape
    tm, _, _, _ = _tiles(seq, n_tok)
    has_moe = moe_prev is not None
    per_b = seq // tm
    nch = d // LANES
    win_w = win.shape[1]

    def tok(i):
        return (i, 0)

    def modmap(i):
        return (i // per_b, 0, 0)

    def const2(i):
        return (0, 0)

    def tabmap(i):
        return (i % per_b, 0)

    in_specs = [pl.BlockSpec((tm, d), tok)]
    args = [x]
    if has_moe:
        in_specs += [pl.BlockSpec((tm * nch, LANES), tok),
                     pl.BlockSpec((None, N_MOD, d), modmap)]
        args += [moe_prev, mod_prev]
    in_specs += [
        pl.BlockSpec((None, N_MOD, d), modmap),
        pl.BlockSpec((1, d), const2),
        pl.BlockSpec((d, win_w), const2),
        pl.BlockSpec((1, Q_LORA_RANK), const2),
        pl.BlockSpec((Q_LORA_RANK, N_HEADS * HEAD_PAD), const2),
        pl.BlockSpec((1, KV_LORA_RANK), const2),
        pl.BlockSpec((KV_LORA_RANK, N_HEADS * HEAD_PAD), const2),
        pl.BlockSpec((KV_LORA_RANK, N_HEADS * V_HEAD_DIM), const2),
        pl.BlockSpec((tm, HEAD_PAD), tabmap),
        pl.BlockSpec((tm, HEAD_PAD), tabmap),
        pl.BlockSpec((tm, HEAD_PAD), tabmap),
        pl.BlockSpec((F_WIDTH, F_WIDTH), const2),
        pl.BlockSpec((F_WIDTH, F_WIDTH), const2),
    ]
    args += [mod, gmix, win, gq, wuq, gkv, wkn, wv, tabs[0], tabs[1], tabs[2], bdc, bds]

    hv = N_HEADS * V_HEAD_DIM
    out_specs = [
        pl.BlockSpec((N_HEADS, tm, HEAD_PAD), lambda i: (0, i, 0)),
        pl.BlockSpec((N_HEADS, tm, HEAD_PAD), lambda i: (0, i, 0)),
        pl.BlockSpec((None, hv, tm), lambda i: (i, 0, 0)),
        pl.BlockSpec((2, tm, F_WIDTH), lambda i: (0, i, 0)),
        pl.BlockSpec((tm, 2 * d), tok),
    ]
    out_shape = [
        jax.ShapeDtypeStruct((N_HEADS, n_tok, HEAD_PAD), BF16),
        jax.ShapeDtypeStruct((N_HEADS, n_tok, HEAD_PAD), BF16),
        jax.ShapeDtypeStruct((n_tok // tm, hv, tm), BF16),
        jax.ShapeDtypeStruct((2, n_tok, F_WIDTH), BF16),
        jax.ShapeDtypeStruct((n_tok, 2 * d), BF16),
    ]
    if has_moe:
        out_specs.append(pl.BlockSpec((tm, d), tok))
        out_shape.append(jax.ShapeDtypeStruct((n_tok, d), F32))
    outs = pl.pallas_call(
        functools.partial(_in_proj_kernel, d=d, has_moe=has_moe),
        grid=(n_tok // tm,),
        in_specs=in_specs,
        out_specs=out_specs,
        out_shape=out_shape,
        compiler_params=_cparams(("arbitrary",)),
        name="in_proj",
    )(*args)
    if has_moe:
        return outs
    return (*outs, x)


def _attn_kernel(q_ref, k_ref, vt_ref, o_ref, sa_ref, sb_ref, *, tk, n_kv):
    q = q_ref[...]
    tq = q.shape[0]
    ones = jnp.ones((ATTN_SUM_ROWS, tk), BF16)

    def scores(j):
        return _dot_nt(k_ref[j * tk:(j + 1) * tk, :], q)

    def update(j, s, m, acc):
        m_new = jnp.maximum(m, jnp.max(s, axis=0, keepdims=True))
        p = jnp.exp2(s - m_new).astype(BF16)
        alpha = jnp.exp2(m - m_new)
        v_ext = jnp.concatenate([vt_ref[j], ones], axis=0)
        return m_new, alpha * acc + _dot(v_ext, p)

    bufs = (sa_ref, sb_ref)
    m = jnp.full((1, tq), NEG_BIG, F32)
    acc = jnp.zeros((V_HEAD_DIM + ATTN_SUM_ROWS, tq), F32)
    bufs[0][...] = scores(0)
    for j in range(n_kv):
        if j + 1 < n_kv:
            bufs[(j + 1) % 2][...] = scores(j + 1)
        m, acc = update(j, bufs[j % 2][...], m, acc)
    o_ref[...] = (acc[0:V_HEAD_DIM] / acc[V_HEAD_DIM:V_HEAD_DIM + 1]).astype(BF16)


def _attention(q, k, vt, *, bsz, seq):
    n_tok = q.shape[1]
    _, tq, tk, _ = _tiles(seq, n_tok)
    n_q = seq // tq
    n_kv = seq // tk
    return pl.pallas_call(
        functools.partial(_attn_kernel, tk=tk, n_kv=n_kv),
        grid=(bsz, N_HEADS, n_q),
        in_specs=[
            pl.BlockSpec((None, tq, HEAD_PAD), lambda b, h, i: (h, b * n_q + i, 0)),
            pl.BlockSpec((None, seq, HEAD_PAD), lambda b, h, i: (h, b, 0)),
            pl.BlockSpec((n_kv, V_HEAD_DIM, tk), lambda b, h, i: (b, h, 0)),
        ],
        out_specs=pl.BlockSpec((V_HEAD_DIM, tq), lambda b, h, i: (h, b * n_q + i)),
        out_shape=jax.ShapeDtypeStruct((N_HEADS * V_HEAD_DIM, n_tok), BF16),
        scratch_shapes=[pltpu.VMEM((tk, tq), F32), pltpu.VMEM((tk, tq), F32)],
        compiler_params=_cparams(("arbitrary", "arbitrary", "arbitrary")),
        name="attention",
    )(q, k, vt)


def _fft1_kernel(x_ref, f_ref, tr_ref, ti_ref, o_ref):
    n1 = tr_ref.shape[0]
    a = _dot(f_ref[...], x_ref[...])
    ar, ai = a[:n1], a[n1:]
    tr, ti = tr_ref[...], ti_ref[...]
    o_ref[0, :, :] = (ar * tr - ai * ti).astype(BF16)
    o_ref[1, :, :] = (ar * ti + ai * tr).astype(BF16)


def _fft2_kernel(x_ref, g_ref, o_ref):
    o_ref[...] = _dot(g_ref[...], x_ref[...]).astype(BF16)


def _seq_dft_real(w, consts, *, bsz, seq):
    f1, tr, ti, g2 = consts
    n2 = FFT_N2
    n1 = seq // n2
    fw = w.shape[2]
    lanes = n2 * fw
    lt = min(4096, lanes)
    xw = w.reshape(2, bsz, n1, lanes).transpose(1, 0, 2, 3).reshape(bsz, 2 * n1, lanes)
    b1 = pl.pallas_call(
        _fft1_kernel,
        grid=(bsz, lanes // lt),
        in_specs=[
            pl.BlockSpec((None, 2 * n1, lt), lambda b, j: (b, 0, j)),
            pl.BlockSpec((2 * n1, 2 * n1), lambda b, j: (0, 0)),
            pl.BlockSpec((n1, lt), lambda b, j: (0, j)),
            pl.BlockSpec((n1, lt), lambda b, j: (0, j)),
        ],
        out_specs=pl.BlockSpec((None, 2, n1, lt), lambda b, j: (b, 0, 0, j)),
        out_shape=jax.ShapeDtypeStruct((bsz, 2, n1, lanes), BF16),
        compiler_params=_cparams(("arbitrary", "arbitrary")),
        name="fft_stage1",
    )(xw, f1, tr, ti)
    lanes2 = n1 * fw
    lt2 = min(4096, lanes2)
    bt = b1.reshape(bsz, 2, n1, n2, fw).transpose(0, 1, 3, 2, 4).reshape(bsz, 2 * n2, lanes2)
    y = pl.pallas_call(
        _fft2_kernel,
        grid=(bsz, lanes2 // lt2),
        in_specs=[
            pl.BlockSpec((None, 2 * n2, lt2), lambda b, j: (b, 0, j)),
            pl.BlockSpec((n2, 2 * n2), lambda b, j: (0, 0)),
        ],
        out_specs=pl.BlockSpec((None, n2, lt2), lambda b, j: (b, 0, j)),
        out_shape=jax.ShapeDtypeStruct((bsz, n2, lanes2), BF16),
        compiler_params=_cparams(("arbitrary", "arbitrary")),
        name="fft_stage2",
    )(bt, g2)
    return y.reshape(bsz * seq, fw)


def _dft_consts(seq):
    n2 = FFT_N2
    n1 = seq // n2
    k1 = np.arange(n1)
    ang1 = 2.0 * np.pi * np.outer(k1, k1) / n1
    c1, s1 = np.cos(ang1), np.sin(ang1)
    f1 = np.block([[c1, s1], [-s1, c1]])
    k2 = np.arange(n2)
    ang2 = 2.0 * np.pi * np.outer(k2, k2) / n2
    scale = 1.0 / math.sqrt(seq * F_GROUP_DIM)
    g2 = np.concatenate([np.cos(ang2), np.sin(ang2)], axis=1) * scale
    prod = (jnp.arange(n1, dtype=jnp.int32)[:, None] * jnp.arange(n2, dtype=jnp.int32)[None, :]) % seq
    ang = prod.astype(F32) * (2.0 * math.pi / seq)
    tr = jnp.repeat(jnp.cos(ang), F_WIDTH, axis=1)
    ti = jnp.repeat(-jnp.sin(ang), F_WIDTH, axis=1)
    return jnp.asarray(f1, BF16), tr, ti, jnp.asarray(g2, BF16)


def _channel_dft_consts():
    j = np.arange(F_GROUP_DIM)
    ang = 2.0 * np.pi * np.outer(j, j) / F_GROUP_DIM
    eye = np.eye(F_GROUPS)
    return jnp.asarray(np.kron(eye, np.cos(ang)), BF16), jnp.asarray(np.kron(eye, np.sin(ang)), BF16)


def _rope_tabs(seq):
    inv = 1.0 / (ROPE_THETA ** (jnp.arange(0, QK_ROPE_DIM, 2, dtype=F32) / QK_ROPE_DIM))
    ang = jnp.arange(seq, dtype=F32)[:, None] * inv[None, :]
    cos, sin = jnp.cos(ang), jnp.sin(ang)
    ones = jnp.ones((seq, QK_NOPE_DIM), F32)
    zeros = jnp.zeros((seq, QK_NOPE_DIM), F32)
    pad1 = jnp.ones((seq, HEAD_PAD - QK_DIM), F32)
    pad0 = jnp.zeros((seq, HEAD_PAD - QK_DIM), F32)
    zr = jnp.zeros_like(sin)
    ctab = jnp.concatenate([ones, cos, cos, pad1], axis=1)
    s1tab = jnp.concatenate([zeros, -sin, zr, pad0], axis=1)
    s2tab = jnp.concatenate([zeros, zr, sin, pad0], axis=1)
    return ctab, s1tab, s2tab


def _post_kernel(ot_ref, yf_ref, gates_ref, x_ref, mod_ref, wa_ref, wb_ref, wout_ref, gffn_ref,
                 wrh_ref, wrl_ref, br_ref, x1_ref, h2p_ref, ti_ref, tg_ref, *, d):
    tm = x_ref.shape[0]
    ya = _dot_tn(ot_ref[...], wa_ref[...])
    yb = _dot(yf_ref[...], wb_ref[...])
    merged = gates_ref[:, 0:d].astype(F32) * ya + gates_ref[:, d:2 * d].astype(F32) * yb
    x1 = x_ref[...] + mod_ref[2:3, :] * _dot(merged.astype(BF16), wout_ref[...])
    x1_ref[...] = x1
    h2 = (_rms(x1) * gffn_ref[...]) * (1.0 + mod_ref[4:5, :]) + mod_ref[3:4, :]

    h_hi, h_lo = _split_bf16(h2)
    logits = _dot(h_hi, wrh_ref[...]) + _dot(h_hi, wrl_ref[...]) + _dot(h_lo, wrh_ref[...]) + br_ref[...]
    lane = lax.broadcasted_iota(jnp.int32, logits.shape, 1)
    work = logits
    vals, idxs = [], []
    for _ in range(TOP_K):
        mx = jnp.max(work, axis=1, keepdims=True)
        ix = jnp.min(jnp.where(work == mx, lane, LANES), axis=1, keepdims=True)
        vals.append(mx)
        idxs.append(ix)
        work = jnp.where(lane == ix, -jnp.inf, work)
    exps = [jnp.exp(v - vals[0]) for v in vals]
    tot = exps[0]
    for e in exps[1:]:
        tot = tot + e
    ti = jnp.zeros(logits.shape, jnp.int32)
    tg = jnp.zeros(logits.shape, F32)
    for kk in range(TOP_K):
        ti = jnp.where(lane == kk, idxs[kk], ti)
        tg = jnp.where(lane == kk, exps[kk] / tot, tg)
    ti_ref[...] = ti
    tg_ref[...] = tg

    bits = lax.bitcast_convert_type(h2.astype(BF16).astype(F32), jnp.uint32)
    half = d // 2
    h2p_ref[...] = _pack_rows((bits[:, 0:half] >> 16) | (bits[:, half:d] & jnp.uint32(0xFFFF0000)))


def _post_mix(ot, yf, gates, x, mod, wa, wb, wout, gffn, wrh, wrl, br, *, bsz, seq):
    n_tok, d = x.shape
    tm, _, _, _ = _tiles(seq, n_tok)
    per_b = seq // tm
    hv = N_HEADS * V_HEAD_DIM
    nchp = d // 2 // LANES

    def tok(i):
        return (i, 0)

    def const2(i):
        return (0, 0)

    return pl.pallas_call(
        functools.partial(_post_kernel, d=d),
        grid=(n_tok // tm,),
        in_specs=[
            pl.BlockSpec((hv, tm), lambda i: (0, i)),
            pl.BlockSpec((tm, F_WIDTH), tok),
            pl.BlockSpec((tm, 2 * d), tok),
            pl.BlockSpec((tm, d), tok),
            pl.BlockSpec((None, N_MOD, d), lambda i: (i // per_b, 0, 0)),
            pl.BlockSpec((hv, d), const2),
            pl.BlockSpec((F_WIDTH, d), const2),
            pl.BlockSpec((d, d), const2),
            pl.BlockSpec((1, d), const2),
            pl.BlockSpec((d, LANES), const2),
            pl.BlockSpec((d, LANES), const2),
            pl.BlockSpec((1, LANES), const2),
        ],
        out_specs=[
            pl.BlockSpec((tm, d), tok),
            pl.BlockSpec((tm * nchp, LANES), tok),
            pl.BlockSpec((tm, LANES), tok),
            pl.BlockSpec((tm, LANES), tok),
        ],
        out_shape=[
            jax.ShapeDtypeStruct((n_tok, d), F32),
            jax.ShapeDtypeStruct((n_tok * nchp, LANES), jnp.uint32),
            jax.ShapeDtypeStruct((n_tok, LANES), jnp.int32),
            jax.ShapeDtypeStruct((n_tok, LANES), F32),
        ],
        compiler_params=_cparams(("arbitrary",)),
        name="post_mix",
    )(ot, yf, gates, x, mod, wa, wb, wout, gffn, wrh, wrl, br)


def _moe_kernel(bexp_ref, bval_ref, tok_ref, gate_ref, h2p_ref, wg_ref, wu_ref, wd_ref, bg_ref, bu_ref, bd_ref,
                out_ref, xs_ref, ys_ref, *, n_blk, tc):
    c = pl.program_id(0)
    j = pl.program_id(1)
    nchp = h2p_ref.shape[0] // tc
    nc = out_ref.shape[0] // tc
    rows = MOE_ROWS

    @pl.when(j == 0)
    def _():
        out_ref[...] = jnp.zeros_like(out_ref)

    @pl.when(bval_ref[c * n_blk + j] > 0)
    def _():
        src2, out2, ys2 = h2p_ref, out_ref, ys_ref
        gp = nchp * SUBLANES
        go = nc * SUBLANES
        for m in range(rows):
            t = tok_ref[0, 0, m]
            xs_ref[pl.ds((m >> 3) * gp + (m & 7), nchp, stride=SUBLANES), :] = (
                src2[pl.ds((t >> 3) * gp + (t & 7), nchp, stride=SUBLANES), :])
        xs4 = xs_ref[...].reshape(rows // SUBLANES, nchp, SUBLANES, LANES)
        los, his = [], []
        for cc in range(nchp):
            wd = xs4[:, cc].reshape(rows, LANES)
            los.append(lax.bitcast_convert_type(wd << 16, F32))
            his.append(lax.bitcast_convert_type(wd & jnp.uint32(0xFFFF0000), F32))
        x = jnp.concatenate(los + his, axis=1).astype(BF16)
        g = jnp.minimum(_dot(x, wg_ref[...]) + bg_ref[...], SWIGLU_LIMIT)
        u = jnp.clip(_dot(x, wu_ref[...]) + bu_ref[...], -SWIGLU_LIMIT, SWIGLU_LIMIT)
        act = ((u + 1.0) * (g * jax.nn.sigmoid(g * SWIGLU_ALPHA))).astype(BF16)
        y = _dot(act, wd_ref[...]) + bd_ref[...]
        ys_ref[...] = _pack_rows(y)
        batch = 4
        for m0 in range(0, rows, batch):
            pend = []
            for m in range(m0, m0 + batch):
                t = tok_ref[0, 0, m]
                base = (t >> 3) * go + (t & 7)
                rv = ys2[pl.ds((m >> 3) * go + (m & 7), nc, stride=SUBLANES), :]
                pend.append((base, out2[pl.ds(base, nc, stride=SUBLANES), :] + rv * gate_ref[0, 0, m]))
            for base, val in pend:
                out2[pl.ds(base, nc, stride=SUBLANES), :] = val


def _moe_ffn(h2p, top_i, top_g, wg, wu, wd, bg, bu, bd, *, d):
    n_tok = top_i.shape[0]
    nchp = h2p.shape[0] // n_tok
    _, _, _, tc = _tiles(n_tok, n_tok)
    nc = d // LANES
    rows = MOE_ROWS
    n_chunks = n_tok // tc
    n_asg = tc * TOP_K
    n_blk = n_asg // rows + N_EXPERTS
    n_exp, _, ff = wg.shape

    e_flat = top_i.reshape(n_chunks, n_asg)
    g_flat = top_g.reshape(n_chunks, n_asg)
    key = e_flat * n_asg + jnp.arange(n_asg, dtype=jnp.int32)[None, :]
    key = jnp.sort(key, axis=1)
    order = key % n_asg
    counts = jnp.sum((e_flat[:, :, None] == jnp.arange(n_exp, dtype=jnp.int32)[None, None, :]).astype(jnp.int32),
                     axis=1)
    start = jnp.cumsum(counts, axis=1) - counts
    nb_e = (counts + rows - 1) // rows
    blk_end = jnp.cumsum(nb_e, axis=1)
    blk_start = blk_end - nb_e
    jb = jnp.arange(n_blk, dtype=jnp.int32)
    e_of_blk = jnp.sum((jb[None, :, None] >= blk_end[:, None, :]).astype(jnp.int32), axis=2)
    blk_valid = e_of_blk < n_exp
    last_e = jnp.max(jnp.where(counts > 0, jnp.arange(n_exp, dtype=jnp.int32)[None, :], 0), axis=1, keepdims=True)
    e_blk = jnp.where(blk_valid, jnp.minimum(e_of_blk, n_exp - 1), last_e)
    q0 = (jb[None, :] - jnp.take_along_axis(blk_start, e_blk, axis=1)) * rows
    cnt_blk = jnp.take_along_axis(counts, e_blk, axis=1)
    n_valid = jnp.where(blk_valid, jnp.clip(cnt_blk - q0, 0, rows), 0)
    r = jnp.arange(rows, dtype=jnp.int32)
    q = q0[:, :, None] + r[None, None, :]
    row_ok = r[None, None, :] < n_valid[:, :, None]
    s_idx = jnp.clip(jnp.take_along_axis(start, e_blk, axis=1)[:, :, None] + q, 0, n_asg - 1)
    a_idx = jnp.take_along_axis(order, s_idx.reshape(n_chunks, n_blk * rows), axis=1)
    row_tok = jnp.where(row_ok.reshape(n_chunks, -1), a_idx // TOP_K, 0).astype(jnp.int32)
    row_gate = jnp.where(row_ok.reshape(n_chunks, -1), jnp.take_along_axis(g_flat, a_idx, axis=1), 0.0)
    row_tok = row_tok.reshape(n_chunks * n_blk, 1, rows)
    row_gate = row_gate.reshape(n_chunks * n_blk, 1, rows).astype(F32)
    bexp = e_blk.reshape(-1).astype(jnp.int32)
    bval = n_valid.reshape(-1).astype(jnp.int32)

    def wmap(c, j, bexp_ref, bval_ref):
        return (bexp_ref[c * n_blk + j], 0, 0)

    def rowmap(c, j, bexp_ref, bval_ref):
        return (c * n_blk + j, 0, 0)

    def chunkmap(c, j, bexp_ref, bval_ref):
        return (c, 0)

    grid_spec = pltpu.PrefetchScalarGridSpec(
        num_scalar_prefetch=2,
        grid=(n_chunks, n_blk),
        in_specs=[
            pl.BlockSpec((1, 1, rows), rowmap, memory_space=pltpu.SMEM),
            pl.BlockSpec((1, 1, rows), rowmap, memory_space=pltpu.SMEM),
            pl.BlockSpec((tc * nchp, LANES), chunkmap, pipeline_mode=pl.Buffered(1)),
            pl.BlockSpec((None, d, ff), wmap),
            pl.BlockSpec((None, d, ff), wmap),
            pl.BlockSpec((None, ff, d), wmap),
            pl.BlockSpec((None, 1, ff), wmap),
            pl.BlockSpec((None, 1, ff), wmap),
            pl.BlockSpec((None, 1, d), wmap),
        ],
        out_specs=pl.BlockSpec((tc * nc, LANES), chunkmap),
        scratch_shapes=[
            pltpu.VMEM((rows * nchp, LANES), jnp.uint32),
            pltpu.VMEM((rows * nc, LANES), F32),
        ],
    )
    return pl.pallas_call(
        functools.partial(_moe_kernel, n_blk=n_blk, tc=tc),
        grid_spec=grid_spec,
        out_shape=jax.ShapeDtypeStruct((n_tok * nc, LANES), F32),
        compiler_params=_cparams(("arbitrary", "arbitrary")),
        name="moe_ffn",
    )(bexp, bval, row_tok, row_gate, h2p, wg, wu, wd, bg, bu, bd)


def _final_kernel(x_ref, moe_ref, mod_ref, g_ref, o_ref):
    tm = x_ref.shape[0]
    x = x_ref[...] + mod_ref[5:6, :] * _unpack_rows(moe_ref, tm)
    o_ref[...] = _rms(x) * g_ref[...]


def _final_norm(x, moe, mod, g_final, *, bsz, seq):
    n_tok, d = x.shape
    tm, _, _, _ = _tiles(seq, n_tok)
    per_b = seq // tm
    nch = d // LANES
    return pl.pallas_call(
        _final_kernel,
        grid=(n_tok // tm,),
        in_specs=[
            pl.BlockSpec((tm, d), lambda i: (i, 0)),
            pl.BlockSpec((tm * nch, LANES), lambda i: (i, 0)),
            pl.BlockSpec((None, N_MOD, d), lambda i: (i // per_b, 0, 0)),
            pl.BlockSpec((1, d), lambda i: (0, 0)),
        ],
        out_specs=pl.BlockSpec((tm, d), lambda i: (i, 0)),
        out_shape=jax.ShapeDtypeStruct((n_tok, d), F32),
        compiler_params=_cparams(("arbitrary",)),
        name="final_norm",
    )(x, moe, mod, g_final)


def _deinterleave_kernel(w_ref, p_ref, g_ref, u_ref):
    for kg in range(w_ref.shape[1] // (2 * LANES)):
        blk = w_ref[:, kg * 2 * LANES:(kg + 1) * 2 * LANES].astype(BF16)
        sel = _dot(blk, p_ref[...])
        g_ref[:, kg * LANES:(kg + 1) * LANES] = sel[:, 0:LANES].astype(BF16)
        u_ref[:, kg * LANES:(kg + 1) * LANES] = sel[:, LANES:2 * LANES].astype(BF16)


def _deinterleave_gate_up(w_gu):
    n_layers, n_exp, d, ff2 = w_gu.shape
    rows = n_layers * n_exp * d
    tr = min(1024, rows)
    sel = np.zeros((2 * LANES, 2 * LANES), np.float32)
    sel[2 * np.arange(LANES), np.arange(LANES)] = 1.0
    sel[2 * np.arange(LANES) + 1, LANES + np.arange(LANES)] = 1.0
    g, u = pl.pallas_call(
        _deinterleave_kernel,
        grid=(rows // tr,),
        in_specs=[pl.BlockSpec((tr, ff2), lambda i: (i, 0)),
                  pl.BlockSpec((2 * LANES, 2 * LANES), lambda i: (0, 0))],
        out_specs=[pl.BlockSpec((tr, ff2 // 2), lambda i: (i, 0)),
                   pl.BlockSpec((tr, ff2 // 2), lambda i: (i, 0))],
        out_shape=[jax.ShapeDtypeStruct((rows, ff2 // 2), BF16),
                   jax.ShapeDtypeStruct((rows, ff2 // 2), BF16)],
        compiler_params=_cparams(("arbitrary",)),
        name="deinterleave_gate_up",
    )(w_gu.reshape(rows, ff2), jnp.asarray(sel, BF16))
    shape = (n_layers, n_exp, d, ff2 // 2)
    return g.reshape(shape), u.reshape(shape)


def _prep_weights(w_in, w_uq, w_ukv, w_a, w_b, w_out, w_router, b_router, w_gu, b_gu, w_dn, b_dn):
    n_layers, d, _ = w_in.shape
    o1 = Q_LORA_RANK
    o2 = o1 + KV_LORA_RANK
    o3 = o2 + QK_ROPE_DIM
    kr = jnp.zeros((n_layers, d, KR_PAD), w_in.dtype).at[:, :, QK_NOPE_DIM:QK_DIM].set(w_in[:, :, o2:o3])
    win = jnp.concatenate([w_in[:, :, :o2], kr, w_in[:, :, o3:]], axis=2).astype(BF16)
    wuq = jnp.pad(w_uq.reshape(n_layers, Q_LORA_RANK, N_HEADS, QK_DIM),
                  ((0, 0), (0, 0), (0, 0), (0, HEAD_PAD - QK_DIM)))
    wuq = wuq.reshape(n_layers, Q_LORA_RANK, N_HEADS * HEAD_PAD).astype(BF16)
    wkv = w_ukv.reshape(n_layers, KV_LORA_RANK, N_HEADS, QK_NOPE_DIM + V_HEAD_DIM)
    wkn = jnp.pad(wkv[..., :QK_NOPE_DIM], ((0, 0), (0, 0), (0, 0), (0, HEAD_PAD - QK_NOPE_DIM)))
    wkn = wkn.reshape(n_layers, KV_LORA_RANK, N_HEADS * HEAD_PAD).astype(BF16)
    wv = wkv[..., QK_NOPE_DIM:].reshape(n_layers, KV_LORA_RANK, N_HEADS * V_HEAD_DIM).astype(BF16)
    n_exp = w_router.shape[2]
    wr = jnp.pad(w_router, ((0, 0), (0, 0), (0, LANES - n_exp)))
    wr_hi = wr.astype(BF16)
    wr_lo = (wr - wr_hi.astype(F32)).astype(BF16)
    br = jnp.pad(b_router, ((0, 0), (0, LANES - n_exp)), constant_values=NEG_BIG)[:, None, :]
    wg, wu = _deinterleave_gate_up(w_gu)
    bg = b_gu[..., 0::2][:, :, None, :]
    bu = b_gu[..., 1::2][:, :, None, :]
    wd = w_dn.astype(BF16)
    bd = b_dn[:, :, None, :]
    return dict(win=win, wuq=wuq, wkn=wkn, wv=wv, wa=w_a.astype(BF16), wb=w_b.astype(BF16),
                wout=w_out.astype(BF16), wr_hi=wr_hi, wr_lo=wr_lo, br=br, wg=wg, wu=wu, bg=bg, bu=bu, wd=wd, bd=bd)


def _trunk(x3, mod, wts, g_mix, g_ffn, g_q, g_kv, g_final, bdc, bds):
    bsz, seq, d = x3.shape
    n_layers = g_mix.shape[0]
    x = x3.reshape(bsz * seq, d)
    tabs = _rope_tabs(seq)
    dft = _dft_consts(seq)
    moe = None
    for l in range(n_layers):
        mod_prev = mod[l - 1] if l > 0 else None
        q, k, vt, w, gates, x = _in_proj(
            x, moe, mod_prev, mod[l], g_mix[l][None, :], wts["win"][l], g_q[l][None, :], wts["wuq"][l],
            g_kv[l][None, :], wts["wkn"][l], wts["wv"][l], tabs, bdc, bds, bsz=bsz, seq=seq)
        ot = _attention(q, k, vt, bsz=bsz, seq=seq)
        yf = _seq_dft_real(w, dft, bsz=bsz, seq=seq)
        x, h2p, top_i, top_g = _post_mix(
            ot, yf, gates, x, mod[l], wts["wa"][l], wts["wb"][l], wts["wout"][l], g_ffn[l][None, :],
            wts["wr_hi"][l], wts["wr_lo"][l], wts["br"][l], bsz=bsz, seq=seq)
        moe = _moe_ffn(h2p, top_i[:, :TOP_K], top_g[:, :TOP_K], wts["wg"][l], wts["wu"][l], wts["wd"][l],
                       wts["bg"][l], wts["bu"][l], wts["bd"][l], d=d)
    y = _final_norm(x, moe, mod[n_layers - 1], g_final[None, :], bsz=bsz, seq=seq)
    return y.reshape(bsz, seq, d)


def kernel(x_prompt, x_sample, c_prompt, c_sample, w_ada, b_ada, g_mix, g_ffn, w_in, g_q, w_uq, g_kv, w_ukv, w_a, w_b, w_out, w_router, b_router, w_gu, b_gu, w_dn, b_dn, g_final):
    n_layers, d, _ = w_ada.shape
    bp, bs = c_prompt.shape[0], c_sample.shape[0]
    rows = -(-(bp + bs) // SUBLANES) * SUBLANES
    c_all = jnp.concatenate([c_prompt, c_sample, jnp.zeros((rows - bp - bs, d), c_prompt.dtype)], axis=0)
    mod = _adaln_mod(c_all, w_ada, b_ada).reshape(n_layers, rows, N_MOD, d)
    wts = _prep_weights(w_in, w_uq, w_ukv, w_a, w_b, w_out, w_router, b_router, w_gu, b_gu, w_dn, b_dn)
    bdc, bds = _channel_dft_consts()
    y_prompt = _trunk(x_prompt, mod[:, :bp], wts, g_mix, g_ffn, g_q, g_kv, g_final, bdc, bds)
    y_sample = _trunk(x_sample, mod[:, bp:bp + bs], wts, g_mix, g_ffn, g_q, g_kv, g_final, bdc, bds)
    return (y_prompt, y_sample)
```

```python
import functools
import math

import numpy as np
import jax
import jax.numpy as jnp
from jax import lax
from jax.experimental import pallas as pl
from jax.experimental.pallas import tpu as pltpu

N_HEADS = 8
QK_NOPE_DIM = 64
QK_ROPE_DIM = 32
V_HEAD_DIM = 64
QK_DIM = QK_NOPE_DIM + QK_ROPE_DIM
Q_LORA_RANK = 384
KV_LORA_RANK = 256
F_GROUPS = 8
F_GROUP_DIM = 64
F_WIDTH = F_GROUPS * F_GROUP_DIM
N_EXPERTS = 32
TOP_K = 4
SWIGLU_LIMIT = 7.0
SWIGLU_ALPHA = 1.702
ROPE_THETA = 10000.0
RMS_EPS = 1e-6
N_MOD = 6

LANES = 128
SUBLANES = 8
VMEM_LIMIT_BYTES = 60 * 1024 * 1024

HEAD_PAD = LANES
KR_PAD = LANES
ROPE_HALF = QK_ROPE_DIM // 2
FFT_N2 = 64
MOE_ROWS = 128
ATTN_SUM_ROWS = 16
LOG2_E = math.log2(math.e)
NEG_BIG = -1e30

F32 = jnp.float32
BF16 = jnp.bfloat16


def _tiles(seq, n_tok):
    tk = min(512, seq // 2)
    tm = tk
    tq = min(512, seq)
    tc = min(4096, n_tok)
    return tm, tq, tk, tc


def _cparams(sem, flags=None):
    return pltpu.CompilerParams(dimension_semantics=sem, vmem_limit_bytes=VMEM_LIMIT_BYTES, flags=flags)


def _dot(a, b):
    return jnp.dot(a, b, preferred_element_type=F32)


def _dot_nt(a, b):
    return lax.dot_general(a, b, (((1,), (1,)), ((), ())), preferred_element_type=F32)


def _dot_tn(a, b):
    return lax.dot_general(a, b, (((0,), (0,)), ((), ())), preferred_element_type=F32)


def _split_bf16(x):
    hi = x.astype(BF16)
    lo = (x - hi.astype(F32)).astype(BF16)
    return hi, lo


def _rms(x, eps=RMS_EPS):
    return x * lax.rsqrt(jnp.mean(x * x, axis=-1, keepdims=True) + eps)


def _mod_kernel(c_ref, w_ref, b_ref, o_ref):
    c = c_ref[...]
    sc = c * jax.nn.sigmoid(c)
    c_hi, c_lo = _split_bf16(sc)
    w_hi, w_lo = _split_bf16(w_ref[...])
    o_ref[...] = _dot(c_hi, w_hi) + _dot(c_hi, w_lo) + _dot(c_lo, w_hi) + b_ref[...]


def _adaln_mod(c_all, w_ada, b_ada):
    n_layers, d, nd = w_ada.shape
    rows = c_all.shape[0]
    tn = min(nd, 1536)
    return pl.pallas_call(
        _mod_kernel,
        grid=(n_layers, nd // tn),
        in_specs=[
            pl.BlockSpec((rows, d), lambda l, j: (0, 0)),
            pl.BlockSpec((None, d, tn), lambda l, j: (l, 0, j)),
            pl.BlockSpec((None, 1, tn), lambda l, j: (l, 0, j)),
        ],
        out_specs=pl.BlockSpec((None, rows, tn), lambda l, j: (l, 0, j)),
        out_shape=jax.ShapeDtypeStruct((n_layers, rows, nd), F32),
        compiler_params=_cparams(("arbitrary", "arbitrary")),
        name="adaln_mod",
    )(c_all, w_ada, b_ada.reshape(n_layers, 1, nd))


def _unpack_rows(blk_ref, n_rows):
    nc = blk_ref.shape[0] // n_rows
    blk = blk_ref[...].reshape(n_rows // SUBLANES, nc, SUBLANES, LANES)
    return jnp.concatenate([blk[:, c].reshape(n_rows, LANES) for c in range(nc)], axis=1)


def _pack_rows(x):
    n_rows, width = x.shape
    nc = width // LANES
    parts = [x[:, c * LANES:(c + 1) * LANES].reshape(n_rows // SUBLANES, SUBLANES, LANES) for c in range(nc)]
    return jnp.stack(parts, axis=1).reshape(n_rows * nc, LANES)


def _rope(x, ctab, s1tab, s2tab):
    width = x.shape[1]
    return x * ctab + pltpu.roll(x, width - ROPE_HALF, 1) * s1tab + pltpu.roll(x, ROPE_HALF, 1) * s2tab


def _in_proj_kernel(*refs, d, has_moe):
    if has_moe:
        (x_ref, moe_ref, modp_ref, mod_ref, gmix_ref, win_ref, gq_ref, wuq_ref, gkv_ref, wkn_ref, wv_ref,
         ctab_ref, s1_ref, s2_ref, bdc_ref, bds_ref,
         q_ref, k_ref, vt_ref, w_ref, gates_ref, x2_ref) = refs
    else:
        (x_ref, mod_ref, gmix_ref, win_ref, gq_ref, wuq_ref, gkv_ref, wkn_ref, wv_ref,
         ctab_ref, s1_ref, s2_ref, bdc_ref, bds_ref,
         q_ref, k_ref, vt_ref, w_ref, gates_ref) = refs
    tm = x_ref.shape[0]
    x = x_ref[...]
    if has_moe:
        x = x + modp_ref[5:6, :] * _unpack_rows(moe_ref, tm)
        x2_ref[...] = x
    h = (_rms(x) * gmix_ref[...]) * (1.0 + mod_ref[1:2, :]) + mod_ref[0:1, :]
    hb = h.astype(BF16)

    o1 = Q_LORA_RANK
    o2 = o1 + KV_LORA_RANK
    o3 = o2 + KR_PAD
    o4 = o3 + F_WIDTH
    o5 = o4 + d

    ctab = jnp.concatenate([ctab_ref[...]] * N_HEADS, axis=1)
    s1tab = jnp.concatenate([s1_ref[...]] * N_HEADS, axis=1)
    s2tab = jnp.concatenate([s2_ref[...]] * N_HEADS, axis=1)

    uq = _dot(hb, win_ref[:, 0:o1])
    rq = (_rms(uq) * gq_ref[...]).astype(BF16)
    q = _rope(_dot(rq, wuq_ref[...]), ctab, s1tab, s2tab) * (QK_DIM ** -0.5 * LOG2_E)
    for hd in range(N_HEADS):
        q_ref[hd, :, :] = q[:, hd * HEAD_PAD:(hd + 1) * HEAD_PAD].astype(BF16)

    ukv = _dot(hb, win_ref[:, o1:o2])
    rkv = (_rms(ukv) * gkv_ref[...]).astype(BF16)
    ukr = _dot(hb, win_ref[:, o2:o3])
    kfull = _dot(rkv, wkn_ref[...]) + jnp.concatenate([ukr] * N_HEADS, axis=1)
    kk = _rope(kfull, ctab, s1tab, s2tab)
    for hd in range(N_HEADS):
        k_ref[hd, :, :] = kk[:, hd * HEAD_PAD:(hd + 1) * HEAD_PAD].astype(BF16)

    v = _dot(rkv, wv_ref[...])
    vt_ref[...] = v.T.astype(BF16)

    ufb = _dot(hb, win_ref[:, o3:o4]).astype(BF16)
    w_ref[0, :, :] = _dot(ufb, bdc_ref[...]).astype(BF16)
    w_ref[1, :, :] = (-_dot(ufb, bds_ref[...])).astype(BF16)

    gates_ref[:, 0:d] = jax.nn.sigmoid(_dot(hb, win_ref[:, o4:o5])).astype(BF16)
    gates_ref[:, d:2 * d] = jax.nn.sigmoid(_dot(hb, win_ref[:, o5:o5 + d])).astype(BF16)


def _in_proj(x, moe_prev, mod_prev, mod, gmix, win, gq, wuq, gkv, wkn, wv, tabs, bdc, bds, *, bsz, seq):
    n_tok, d = x.shape
    tm, _, _, _ = _tiles(seq, n_tok)
    has_moe = moe_prev is not None
    per_b = seq // tm
    nch = d // LANES
    win_w = win.shape[1]

    def tok(i):
        return (i, 0)

    def modmap(i):
        return (i // per_b, 0, 0)

    def const2(i):
        return (0, 0)

    def tabmap(i):
        return (i % per_b, 0)

    in_specs = [pl.BlockSpec((tm, d), tok)]
    args = [x]
    if has_moe:
        in_specs += [pl.BlockSpec((tm * nch, LANES), tok),
                     pl.BlockSpec((None, N_MOD, d), modmap)]
        args += [moe_prev, mod_prev]
    in_specs += [
        pl.BlockSpec((None, N_MOD, d), modmap),
        pl.BlockSpec((1, d), const2),
        pl.BlockSpec((d, win_w), const2),
        pl.BlockSpec((1, Q_LORA_RANK), const2),
        pl.BlockSpec((Q_LORA_RANK, N_HEADS * HEAD_PAD), const2),
        pl.BlockSpec((1, KV_LORA_RANK), const2),
        pl.BlockSpec((KV_LORA_RANK, N_HEADS * HEAD_PAD), const2),
        pl.BlockSpec((KV_LORA_RANK, N_HEADS * V_HEAD_DIM), const2),
        pl.BlockSpec((tm, HEAD_PAD), tabmap),
        pl.BlockSpec((tm, HEAD_PAD), tabmap),
        pl.BlockSpec((tm, HEAD_PAD), tabmap),
        pl.BlockSpec((F_WIDTH, F_WIDTH), const2),
        pl.BlockSpec((F_WIDTH, F_WIDTH), const2),
    ]
    args += [mod, gmix, win, gq, wuq, gkv, wkn, wv, tabs[0], tabs[1], tabs[2], bdc, bds]

    hv = N_HEADS * V_HEAD_DIM
    out_specs = [
        pl.BlockSpec((N_HEADS, tm, HEAD_PAD), lambda i: (0, i, 0)),
        pl.BlockSpec((N_HEADS, tm, HEAD_PAD), lambda i: (0, i, 0)),
        pl.BlockSpec((None, hv, tm), lambda i: (i, 0, 0)),
        pl.BlockSpec((2, tm, F_WIDTH), lambda i: (0, i, 0)),
        pl.BlockSpec((tm, 2 * d), tok),
    ]
    out_shape = [
        jax.ShapeDtypeStruct((N_HEADS, n_tok, HEAD_PAD), BF16),
        jax.ShapeDtypeStruct((N_HEADS, n_tok, HEAD_PAD), BF16),
        jax.ShapeDtypeStruct((n_tok // tm, hv, tm), BF16),
        jax.ShapeDtypeStruct((2, n_tok, F_WIDTH), BF16),
        jax.ShapeDtypeStruct((n_tok, 2 * d), BF16),
    ]
    if has_moe:
        out_specs.append(pl.BlockSpec((tm, d), tok))
        out_shape.append(jax.ShapeDtypeStruct((n_tok, d), F32))
    outs = pl.pallas_call(
        functools.partial(_in_proj_kernel, d=d, has_moe=has_moe),
        grid=(n_tok // tm,),
        in_specs=in_specs,
        out_specs=out_specs,
        out_shape=out_shape,
        compiler_params=_cparams(("arbitrary",)),
        name="in_proj",
    )(*args)
    if has_moe:
        return outs
    return (*outs, x)


def _attn_kernel(q_ref, k_ref, vt_ref, o_ref, sa_ref, sb_ref, *, tk, n_kv):
    q = q_ref[...]
    tq = q.shape[0]
    ones = jnp.ones((ATTN_SUM_ROWS, tk), BF16)

    def scores(j):
        return _dot_nt(k_ref[j * tk:(j + 1) * tk, :], q).astype(BF16)

    def update(j, s, m, acc):
        m_new = jnp.maximum(m, jnp.max(s, axis=0, keepdims=True).astype(F32))
        p = jnp.exp2(s - m_new.astype(BF16))
        alpha = jnp.exp2(m - m_new)
        v_ext = jnp.concatenate([vt_ref[j], ones], axis=0)
        return m_new, alpha * acc + _dot(v_ext, p)

    bufs = (sa_ref, sb_ref)
    m = jnp.full((1, tq), NEG_BIG, F32)
    acc = jnp.zeros((V_HEAD_DIM + ATTN_SUM_ROWS, tq), F32)
    bufs[0][...] = scores(0)
    for j in range(n_kv):
        if j + 1 < n_kv:
            bufs[(j + 1) % 2][...] = scores(j + 1)
        m, acc = update(j, bufs[j % 2][...], m, acc)
    o_ref[...] = (acc[0:V_HEAD_DIM] / acc[V_HEAD_DIM:V_HEAD_DIM + 1]).astype(BF16)


def _attention(q, k, vt, *, bsz, seq):
    n_tok = q.shape[1]
    _, tq, tk, _ = _tiles(seq, n_tok)
    n_q = seq // tq
    n_kv = seq // tk
    return pl.pallas_call(
        functools.partial(_attn_kernel, tk=tk, n_kv=n_kv),
        grid=(bsz, N_HEADS, n_q),
        in_specs=[
            pl.BlockSpec((None, tq, HEAD_PAD), lambda b, h, i: (h, b * n_q + i, 0)),
            pl.BlockSpec((None, seq, HEAD_PAD), lambda b, h, i: (h, b, 0)),
            pl.BlockSpec((n_kv, V_HEAD_DIM, tk), lambda b, h, i: (b, h, 0)),
        ],
        out_specs=pl.BlockSpec((V_HEAD_DIM, tq), lambda b, h, i: (h, b * n_q + i)),
        out_shape=jax.ShapeDtypeStruct((N_HEADS * V_HEAD_DIM, n_tok), BF16),
        scratch_shapes=[pltpu.VMEM((tk, tq), BF16), pltpu.VMEM((tk, tq), BF16)],
        compiler_params=_cparams(("arbitrary", "arbitrary", "arbitrary")),
        name="attention",
    )(q, k, vt)


def _fft1_kernel(x_ref, f_ref, tr_ref, ti_ref, o_ref):
    n1 = tr_ref.shape[0]
    a = _dot(f_ref[...], x_ref[...])
    ar, ai = a[:n1], a[n1:]
    tr, ti = tr_ref[...], ti_ref[...]
    o_ref[0, :, :] = (ar * tr - ai * ti).astype(BF16)
    o_ref[1, :, :] = (ar * ti + ai * tr).astype(BF16)


def _fft2_kernel(x_ref, g_ref, o_ref):
    o_ref[...] = _dot(g_ref[...], x_ref[...]).astype(BF16)


def _seq_dft_real(w, consts, *, bsz, seq):
    f1, tr, ti, g2 = consts
    n2 = FFT_N2
    n1 = seq // n2
    fw = w.shape[2]
    lanes = n2 * fw
    lt = min(4096, lanes)
    xw = w.reshape(2, bsz, n1, lanes).transpose(1, 0, 2, 3).reshape(bsz, 2 * n1, lanes)
    b1 = pl.pallas_call(
        _fft1_kernel,
        grid=(bsz, lanes // lt),
        in_specs=[
            pl.BlockSpec((None, 2 * n1, lt), lambda b, j: (b, 0, j)),
            pl.BlockSpec((2 * n1, 2 * n1), lambda b, j: (0, 0)),
            pl.BlockSpec((n1, lt), lambda b, j: (0, j)),
            pl.BlockSpec((n1, lt), lambda b, j: (0, j)),
        ],
        out_specs=pl.BlockSpec((None, 2, n1, lt), lambda b, j: (b, 0, 0, j)),
        out_shape=jax.ShapeDtypeStruct((bsz, 2, n1, lanes), BF16),
        compiler_params=_cparams(("arbitrary", "arbitrary")),
        name="fft_stage1",
    )(xw, f1, tr, ti)
    lanes2 = n1 * fw
    lt2 = min(4096, lanes2)
    bt = b1.reshape(bsz, 2, n1, n2, fw).transpose(0, 1, 3, 2, 4).reshape(bsz, 2 * n2, lanes2)
    y = pl.pallas_call(
        _fft2_kernel,
        grid=(bsz, lanes2 // lt2),
        in_specs=[
            pl.BlockSpec((None, 2 * n2, lt2), lambda b, j: (b, 0, j)),
            pl.BlockSpec((n2, 2 * n2), lambda b, j: (0, 0)),
        ],
        out_specs=pl.BlockSpec((None, n2, lt2), lambda b, j: (b, 0, j)),
        out_shape=jax.ShapeDtypeStruct((bsz, n2, lanes2), BF16),
        compiler_params=_cparams(("arbitrary", "arbitrary")),
        name="fft_stage2",
    )(bt, g2)
    return y.reshape(bsz * seq, fw)


def _dft_consts(seq):
    n2 = FFT_N2
    n1 = seq // n2
    k1 = np.arange(n1)
    ang1 = 2.0 * np.pi * np.outer(k1, k1) / n1
    c1, s1 = np.cos(ang1), np.sin(ang1)
    f1 = np.block([[c1, s1], [-s1, c1]])
    k2 = np.arange(n2)
    ang2 = 2.0 * np.pi * np.outer(k2, k2) / n2
    scale = 1.0 / math.sqrt(seq * F_GROUP_DIM)
    g2 = np.concatenate([np.cos(ang2), np.sin(ang2)], axis=1) * scale
    prod = (jnp.arange(n1, dtype=jnp.int32)[:, None] * jnp.arange(n2, dtype=jnp.int32)[None, :]) % seq
    ang = prod.astype(F32) * (2.0 * math.pi / seq)
    tr = jnp.repeat(jnp.cos(ang), F_WIDTH, axis=1)
    ti = jnp.repeat(-jnp.sin(ang), F_WIDTH, axis=1)
    return jnp.asarray(f1, BF16), tr, ti, jnp.asarray(g2, BF16)


def _channel_dft_consts():
    j = np.arange(F_GROUP_DIM)
    ang = 2.0 * np.pi * np.outer(j, j) / F_GROUP_DIM
    eye = np.eye(F_GROUPS)
    return jnp.asarray(np.kron(eye, np.cos(ang)), BF16), jnp.asarray(np.kron(eye, np.sin(ang)), BF16)


def _rope_tabs(seq):
    inv = 1.0 / (ROPE_THETA ** (jnp.arange(0, QK_ROPE_DIM, 2, dtype=F32) / QK_ROPE_DIM))
    ang = jnp.arange(seq, dtype=F32)[:, None] * inv[None, :]
    cos, sin = jnp.cos(ang), jnp.sin(ang)
    ones = jnp.ones((seq, QK_NOPE_DIM), F32)
    zeros = jnp.zeros((seq, QK_NOPE_DIM), F32)
    pad1 = jnp.ones((seq, HEAD_PAD - QK_DIM), F32)
    pad0 = jnp.zeros((seq, HEAD_PAD - QK_DIM), F32)
    zr = jnp.zeros_like(sin)
    ctab = jnp.concatenate([ones, cos, cos, pad1], axis=1)
    s1tab = jnp.concatenate([zeros, -sin, zr, pad0], axis=1)
    s2tab = jnp.concatenate([zeros, zr, sin, pad0], axis=1)
    return ctab, s1tab, s2tab


def _post_kernel(ot_ref, yf_ref, gates_ref, x_ref, mod_ref, wa_ref, wb_ref, wout_ref, gffn_ref,
                 wrh_ref, wrl_ref, br_ref, x1_ref, h2p_ref, ti_ref, tg_ref, *, d):
    tm = x_ref.shape[0]
    ya = _dot_tn(ot_ref[...], wa_ref[...])
    yb = _dot(yf_ref[...], wb_ref[...])
    merged = gates_ref[:, 0:d].astype(F32) * ya + gates_ref[:, d:2 * d].astype(F32) * yb
    x1 = x_ref[...] + mod_ref[2:3, :] * _dot(merged.astype(BF16), wout_ref[...])
    x1_ref[...] = x1
    h2 = (_rms(x1) * gffn_ref[...]) * (1.0 + mod_ref[4:5, :]) + mod_ref[3:4, :]

    h_hi, h_lo = _split_bf16(h2)
    logits = _dot(h_hi, wrh_ref[...]) + _dot(h_hi, wrl_ref[...]) + _dot(h_lo, wrh_ref[...]) + br_ref[...]
    lane = lax.broadcasted_iota(jnp.int32, logits.shape, 1)
    work = logits
    vals, idxs = [], []
    for _ in range(TOP_K):
        mx = jnp.max(work, axis=1, keepdims=True)
        ix = jnp.min(jnp.where(work == mx, lane, LANES), axis=1, keepdims=True)
        vals.append(mx)
        idxs.append(ix)
        work = jnp.where(lane == ix, -jnp.inf, work)
    exps = [jnp.exp(v - vals[0]) for v in vals]
    tot = exps[0]
    for e in exps[1:]:
        tot = tot + e
    ti = jnp.zeros(logits.shape, jnp.int32)
    tg = jnp.zeros(logits.shape, F32)
    for kk in range(TOP_K):
        ti = jnp.where(lane == kk, idxs[kk], ti)
        tg = jnp.where(lane == kk, exps[kk] / tot, tg)
    ti_ref[...] = ti
    tg_ref[...] = tg

    bits = lax.bitcast_convert_type(h2.astype(BF16).astype(F32), jnp.uint32)
    half = d // 2
    h2p_ref[...] = _pack_rows((bits[:, 0:half] >> 16) | (bits[:, half:d] & jnp.uint32(0xFFFF0000)))


def _post_mix(ot, yf, gates, x, mod, wa, wb, wout, gffn, wrh, wrl, br, *, bsz, seq):
    n_tok, d = x.shape
    tm, _, _, _ = _tiles(seq, n_tok)
    per_b = seq // tm
    hv = N_HEADS * V_HEAD_DIM
    nchp = d // 2 // LANES

    def tok(i):
        return (i, 0)

    def const2(i):
        return (0, 0)

    return pl.pallas_call(
        functools.partial(_post_kernel, d=d),
        grid=(n_tok // tm,),
        in_specs=[
            pl.BlockSpec((hv, tm), lambda i: (0, i)),
            pl.BlockSpec((tm, F_WIDTH), tok),
            pl.BlockSpec((tm, 2 * d), tok),
            pl.BlockSpec((tm, d), tok),
            pl.BlockSpec((None, N_MOD, d), lambda i: (i // per_b, 0, 0)),
            pl.BlockSpec((hv, d), const2),
            pl.BlockSpec((F_WIDTH, d), const2),
            pl.BlockSpec((d, d), const2),
            pl.BlockSpec((1, d), const2),
            pl.BlockSpec((d, LANES), const2),
            pl.BlockSpec((d, LANES), const2),
            pl.BlockSpec((1, LANES), const2),
        ],
        out_specs=[
            pl.BlockSpec((tm, d), tok),
            pl.BlockSpec((tm * nchp, LANES), tok),
            pl.BlockSpec((tm, LANES), tok),
            pl.BlockSpec((tm, LANES), tok),
        ],
        out_shape=[
            jax.ShapeDtypeStruct((n_tok, d), F32),
            jax.ShapeDtypeStruct((n_tok * nchp, LANES), jnp.uint32),
            jax.ShapeDtypeStruct((n_tok, LANES), jnp.int32),
            jax.ShapeDtypeStruct((n_tok, LANES), F32),
        ],
        compiler_params=_cparams(("arbitrary",)),
        name="post_mix",
    )(ot, yf, gates, x, mod, wa, wb, wout, gffn, wrh, wrl, br)


def _moe_kernel(bexp_ref, nv_ref, tokg_ref, toks_ref, gates_ref, h2p_ref, wg_ref, wu_ref, wd_ref,
                bg_ref, bu_ref, bd_ref, out_ref, xa_ref, xb_ref, ya_ref, yb_ref, *, tc):
    c = pl.program_id(0)
    j = pl.program_id(1)
    nchp = h2p_ref.shape[0] // tc
    nc = out_ref.shape[0] // tc
    rows = MOE_ROWS
    gp = nchp * SUBLANES
    go = nc * SUBLANES

    @pl.when(j == 0)
    def _():
        out_ref[...] = jnp.zeros_like(out_ref)
        xb_ref[...] = jnp.zeros_like(xb_ref)
        ya_ref[...] = jnp.zeros_like(ya_ref)
        yb_ref[...] = jnp.zeros_like(yb_ref)

    def step(x_nxt, x_cur, y_cur, y_prv):
        for m in range(rows):
            t = tokg_ref[0, 0, m]
            x_nxt[pl.ds((m >> 3) * gp + (m & 7), nchp, stride=SUBLANES), :] = (
                h2p_ref[pl.ds((t >> 3) * gp + (t & 7), nchp, stride=SUBLANES), :])
        xs4 = x_cur[...].reshape(rows // SUBLANES, nchp, SUBLANES, LANES)
        los, his = [], []
        for cc in range(nchp):
            wd = xs4[:, cc].reshape(rows, LANES)
            los.append(lax.bitcast_convert_type(wd << 16, F32))
            his.append(lax.bitcast_convert_type(wd & jnp.uint32(0xFFFF0000), F32))
        x = jnp.concatenate(los + his, axis=1).astype(BF16)
        g = jnp.minimum(_dot(x, wg_ref[...]) + bg_ref[...], SWIGLU_LIMIT)
        u = jnp.clip(_dot(x, wu_ref[...]) + bu_ref[...], -SWIGLU_LIMIT, SWIGLU_LIMIT)
        act = ((u + 1.0) * (g * jax.nn.sigmoid(g * SWIGLU_ALPHA))).astype(BF16)
        y_cur[...] = _pack_rows(_dot(act, wd_ref[...]) + bd_ref[...])
        batch = 4
        for m0 in range(0, rows, batch):
            pend = []
            for m in range(m0, m0 + batch):
                t = toks_ref[0, 0, m]
                base = (t >> 3) * go + (t & 7)
                rv = y_prv[pl.ds((m >> 3) * go + (m & 7), nc, stride=SUBLANES), :]
                pend.append((base, out_ref[pl.ds(base, nc, stride=SUBLANES), :] + rv * gates_ref[0, 0, m]))
            for base, val in pend:
                out_ref[pl.ds(base, nc, stride=SUBLANES), :] = val

    live = j < nv_ref[c] + 2

    @pl.when(jnp.logical_and(live, j % 2 == 0))
    def _():
        step(xa_ref, xb_ref, yb_ref, ya_ref)

    @pl.when(jnp.logical_and(live, j % 2 == 1))
    def _():
        step(xb_ref, xa_ref, ya_ref, yb_ref)


def _moe_ffn(h2p, top_i, top_g, wg, wu, wd, bg, bu, bd, *, d):
    n_tok = top_i.shape[0]
    nchp = h2p.shape[0] // n_tok
    _, _, _, tc = _tiles(n_tok, n_tok)
    nc = d // LANES
    rows = MOE_ROWS
    n_chunks = n_tok // tc
    n_asg = tc * TOP_K
    n_blk = n_asg // rows + N_EXPERTS
    n_exp, _, ff = wg.shape

    e_flat = top_i.reshape(n_chunks, n_asg)
    g_flat = top_g.reshape(n_chunks, n_asg)
    key = e_flat * n_asg + jnp.arange(n_asg, dtype=jnp.int32)[None, :]
    key = jnp.sort(key, axis=1)
    order = key % n_asg
    counts = jnp.sum((e_flat[:, :, None] == jnp.arange(n_exp, dtype=jnp.int32)[None, None, :]).astype(jnp.int32),
                     axis=1)
    start = jnp.cumsum(counts, axis=1) - counts
    nb_e = (counts + rows - 1) // rows
    blk_end = jnp.cumsum(nb_e, axis=1)
    blk_start = blk_end - nb_e
    jb = jnp.arange(n_blk, dtype=jnp.int32)
    e_of_blk = jnp.sum((jb[None, :, None] >= blk_end[:, None, :]).astype(jnp.int32), axis=2)
    blk_valid = e_of_blk < n_exp
    last_e = jnp.max(jnp.where(counts > 0, jnp.arange(n_exp, dtype=jnp.int32)[None, :], 0), axis=1, keepdims=True)
    e_blk = jnp.where(blk_valid, jnp.minimum(e_of_blk, n_exp - 1), last_e)
    q0 = (jb[None, :] - jnp.take_along_axis(blk_start, e_blk, axis=1)) * rows
    cnt_blk = jnp.take_along_axis(counts, e_blk, axis=1)
    n_valid = jnp.where(blk_valid, jnp.clip(cnt_blk - q0, 0, rows), 0)
    r = jnp.arange(rows, dtype=jnp.int32)
    q = q0[:, :, None] + r[None, None, :]
    row_ok = r[None, None, :] < n_valid[:, :, None]
    s_idx = jnp.clip(jnp.take_along_axis(start, e_blk, axis=1)[:, :, None] + q, 0, n_asg - 1)
    a_idx = jnp.take_along_axis(order, s_idx.reshape(n_chunks, n_blk * rows), axis=1)
    row_tok = jnp.where(row_ok.reshape(n_chunks, -1), a_idx // TOP_K, 0).astype(jnp.int32)
    row_gate = jnp.where(row_ok.reshape(n_chunks, -1), jnp.take_along_axis(g_flat, a_idx, axis=1), 0.0)
    n_steps = n_blk + 2
    row_tok = row_tok.reshape(n_chunks, n_blk, rows)
    row_gate = row_gate.reshape(n_chunks, n_blk, rows).astype(F32)
    pad_i = jnp.zeros((n_chunks, 2, rows), jnp.int32)
    pad_f = jnp.zeros((n_chunks, 2, rows), F32)
    tok_gather = jnp.concatenate([row_tok, pad_i], axis=1).reshape(n_chunks * n_steps, 1, rows)
    tok_scatter = jnp.concatenate([pad_i, row_tok], axis=1).reshape(n_chunks * n_steps, 1, rows)
    gate_scatter = jnp.concatenate([pad_f, row_gate], axis=1).reshape(n_chunks * n_steps, 1, rows)
    e_blk = e_blk.astype(jnp.int32)
    bexp = jnp.concatenate([e_blk[:, :1], e_blk, e_blk[:, -1:]], axis=1).reshape(-1)
    n_live = blk_end[:, -1].astype(jnp.int32)

    def wmap(c, j, bexp_ref, nv_ref):
        return (bexp_ref[c * n_steps + j], 0, 0)

    def rowmap(c, j, bexp_ref, nv_ref):
        return (c * n_steps + j, 0, 0)

    def chunkmap(c, j, bexp_ref, nv_ref):
        return (c, 0)

    grid_spec = pltpu.PrefetchScalarGridSpec(
        num_scalar_prefetch=2,
        grid=(n_chunks, n_steps),
        in_specs=[
            pl.BlockSpec((1, 1, rows), rowmap, memory_space=pltpu.SMEM),
            pl.BlockSpec((1, 1, rows), rowmap, memory_space=pltpu.SMEM),
            pl.BlockSpec((1, 1, rows), rowmap, memory_space=pltpu.SMEM),
            pl.BlockSpec((tc * nchp, LANES), chunkmap, pipeline_mode=pl.Buffered(1)),
            pl.BlockSpec((None, d, ff), wmap),
            pl.BlockSpec((None, d, ff), wmap),
            pl.BlockSpec((None, ff, d), wmap),
            pl.BlockSpec((None, 1, ff), wmap),
            pl.BlockSpec((None, 1, ff), wmap),
            pl.BlockSpec((None, 1, d), wmap),
        ],
        out_specs=pl.BlockSpec((tc * nc, LANES), chunkmap),
        scratch_shapes=[
            pltpu.VMEM((rows * nchp, LANES), jnp.uint32),
            pltpu.VMEM((rows * nchp, LANES), jnp.uint32),
            pltpu.VMEM((rows * nc, LANES), F32),
            pltpu.VMEM((rows * nc, LANES), F32),
        ],
    )
    return pl.pallas_call(
        functools.partial(_moe_kernel, tc=tc),
        grid_spec=grid_spec,
        out_shape=jax.ShapeDtypeStruct((n_tok * nc, LANES), F32),
        compiler_params=_cparams(("arbitrary", "arbitrary")),
        name="moe_ffn",
    )(bexp, n_live, tok_gather, tok_scatter, gate_scatter, h2p, wg, wu, wd, bg, bu, bd)


def _final_kernel(x_ref, moe_ref, mod_ref, g_ref, o_ref):
    tm = x_ref.shape[0]
    x = x_ref[...] + mod_ref[5:6, :] * _unpack_rows(moe_ref, tm)
    o_ref[...] = _rms(x) * g_ref[...]


def _final_norm(x, moe, mod, g_final, *, bsz, seq):
    n_tok, d = x.shape
    tm, _, _, _ = _tiles(seq, n_tok)
    per_b = seq // tm
    nch = d // LANES
    return pl.pallas_call(
        _final_kernel,
        grid=(n_tok // tm,),
        in_specs=[
            pl.BlockSpec((tm, d), lambda i: (i, 0)),
            pl.BlockSpec((tm * nch, LANES), lambda i: (i, 0)),
            pl.BlockSpec((None, N_MOD, d), lambda i: (i // per_b, 0, 0)),
            pl.BlockSpec((1, d), lambda i: (0, 0)),
        ],
        out_specs=pl.BlockSpec((tm, d), lambda i: (i, 0)),
        out_shape=jax.ShapeDtypeStruct((n_tok, d), F32),
        compiler_params=_cparams(("arbitrary",)),
        name="final_norm",
    )(x, moe, mod, g_final)


def _deinterleave_kernel(w_ref, p_ref, g_ref, u_ref):
    for kg in range(w_ref.shape[1] // (2 * LANES)):
        blk = w_ref[:, kg * 2 * LANES:(kg + 1) * 2 * LANES].astype(BF16)
        sel = _dot(blk, p_ref[...])
        g_ref[:, kg * LANES:(kg + 1) * LANES] = sel[:, 0:LANES].astype(BF16)
        u_ref[:, kg * LANES:(kg + 1) * LANES] = sel[:, LANES:2 * LANES].astype(BF16)


def _deinterleave_gate_up(w_gu):
    n_layers, n_exp, d, ff2 = w_gu.shape
    rows = n_layers * n_exp * d
    tr = min(1024, rows)
    sel = np.zeros((2 * LANES, 2 * LANES), np.float32)
    sel[2 * np.arange(LANES), np.arange(LANES)] = 1.0
    sel[2 * np.arange(LANES) + 1, LANES + np.arange(LANES)] = 1.0
    g, u = pl.pallas_call(
        _deinterleave_kernel,
        grid=(rows // tr,),
        in_specs=[pl.BlockSpec((tr, ff2), lambda i: (i, 0)),
                  pl.BlockSpec((2 * LANES, 2 * LANES), lambda i: (0, 0))],
        out_specs=[pl.BlockSpec((tr, ff2 // 2), lambda i: (i, 0)),
                   pl.BlockSpec((tr, ff2 // 2), lambda i: (i, 0))],
        out_shape=[jax.ShapeDtypeStruct((rows, ff2 // 2), BF16),
                   jax.ShapeDtypeStruct((rows, ff2 // 2), BF16)],
        compiler_params=_cparams(("arbitrary",)),
        name="deinterleave_gate_up",
    )(w_gu.reshape(rows, ff2), jnp.asarray(sel, BF16))
    shape = (n_layers, n_exp, d, ff2 // 2)
    return g.reshape(shape), u.reshape(shape)


def _prep_weights(w_in, w_uq, w_ukv, w_a, w_b, w_out, w_router, b_router, w_gu, b_gu, w_dn, b_dn):
    n_layers, d, _ = w_in.shape
    o1 = Q_LORA_RANK
    o2 = o1 + KV_LORA_RANK
    o3 = o2 + QK_ROPE_DIM
    kr = jnp.zeros((n_layers, d, KR_PAD), w_in.dtype).at[:, :, QK_NOPE_DIM:QK_DIM].set(w_in[:, :, o2:o3])
    win = jnp.concatenate([w_in[:, :, :o2], kr, w_in[:, :, o3:]], axis=2).astype(BF16)
    wuq = jnp.pad(w_uq.reshape(n_layers, Q_LORA_RANK, N_HEADS, QK_DIM),
                  ((0, 0), (0, 0), (0, 0), (0, HEAD_PAD - QK_DIM)))
    wuq = wuq.reshape(n_layers, Q_LORA_RANK, N_HEADS * HEAD_PAD).astype(BF16)
    wkv = w_ukv.reshape(n_layers, KV_LORA_RANK, N_HEADS, QK_NOPE_DIM + V_HEAD_DIM)
    wkn = jnp.pad(wkv[..., :QK_NOPE_DIM], ((0, 0), (0, 0), (0, 0), (0, HEAD_PAD - QK_NOPE_DIM)))
    wkn = wkn.reshape(n_layers, KV_LORA_RANK, N_HEADS * HEAD_PAD).astype(BF16)
    wv = wkv[..., QK_NOPE_DIM:].reshape(n_layers, KV_LORA_RANK, N_HEADS * V_HEAD_DIM).astype(BF16)
    n_exp = w_router.shape[2]
    wr = jnp.pad(w_router, ((0, 0), (0, 0), (0, LANES - n_exp)))
    wr_hi = wr.astype(BF16)
    wr_lo = (wr - wr_hi.astype(F32)).astype(BF16)
    br = jnp.pad(b_router, ((0, 0), (0, LANES - n_exp)), constant_values=NEG_BIG)[:, None, :]
    wg, wu = _deinterleave_gate_up(w_gu)
    bg = b_gu[..., 0::2][:, :, None, :]
    bu = b_gu[..., 1::2][:, :, None, :]
    wd = w_dn.astype(BF16)
    bd = b_dn[:, :, None, :]
    return dict(win=win, wuq=wuq, wkn=wkn, wv=wv, wa=w_a.astype(BF16), wb=w_b.astype(BF16),
                wout=w_out.astype(BF16), wr_hi=wr_hi, wr_lo=wr_lo, br=br, wg=wg, wu=wu, bg=bg, bu=bu, wd=wd, bd=bd)


def _trunk(x3, mod, wts, g_mix, g_ffn, g_q, g_kv, g_final, bdc, bds):
    bsz, seq, d = x3.shape
    n_layers = g_mix.shape[0]
    x = x3.reshape(bsz * seq, d)
    tabs = _rope_tabs(seq)
    dft = _dft_consts(seq)
    moe = None
    for l in range(n_layers):
        mod_prev = mod[l - 1] if l > 0 else None
        q, k, vt, w, gates, x = _in_proj(
            x, moe, mod_prev, mod[l], g_mix[l][None, :], wts["win"][l], g_q[l][None, :], wts["wuq"][l],
            g_kv[l][None, :], wts["wkn"][l], wts["wv"][l], tabs, bdc, bds, bsz=bsz, seq=seq)
        ot = _attention(q, k, vt, bsz=bsz, seq=seq)
        yf = _seq_dft_real(w, dft, bsz=bsz, seq=seq)
        x, h2p, top_i, top_g = _post_mix(
            ot, yf, gates, x, mod[l], wts["wa"][l], wts["wb"][l], wts["wout"][l], g_ffn[l][None, :],
            wts["wr_hi"][l], wts["wr_lo"][l], wts["br"][l], bsz=bsz, seq=seq)
        moe = _moe_ffn(h2p, top_i[:, :TOP_K], top_g[:, :TOP_K], wts["wg"][l], wts["wu"][l], wts["wd"][l],
                       wts["bg"][l], wts["bu"][l], wts["bd"][l], d=d)
    y = _final_norm(x, moe, mod[n_layers - 1], g_final[None, :], bsz=bsz, seq=seq)
    return y.reshape(bsz, seq, d)


def kernel(x_prompt, x_sample, c_prompt, c_sample, w_ada, b_ada, g_mix, g_ffn, w_in, g_q, w_uq, g_kv, w_ukv, w_a, w_b, w_out, w_router, b_router, w_gu, b_gu, w_dn, b_dn, g_final):
    n_layers, d, _ = w_ada.shape
    bp, bs = c_prompt.shape[0], c_sample.shape[0]
    rows = -(-(bp + bs) // SUBLANES) * SUBLANES
    c_all = jnp.concatenate([c_prompt, c_sample, jnp.zeros((rows - bp - bs, d), c_prompt.dtype)], axis=0)
    mod = _adaln_mod(c_all, w_ada, b_ada).reshape(n_layers, rows, N_MOD, d)
    wts = _prep_weights(w_in, w_uq, w_ukv, w_a, w_b, w_out, w_router, b_router, w_gu, b_gu, w_dn, b_dn)
    bdc, bds = _channel_dft_consts()
    y_prompt = _trunk(x_prompt, mod[:, :bp], wts, g_mix, g_ffn, g_q, g_kv, g_final, bdc, bds)
    y_sample = _trunk(x_sample, mod[:, bp:bp + bs], wts, g_mix, g_ffn, g_q, g_kv, g_final, bdc, bds)
    return (y_prompt, y_sample)
```

```python
import functools
import math

import numpy as np
import jax
import jax.numpy as jnp
from jax import lax
from jax.experimental import pallas as pl
from jax.experimental.pallas import tpu as pltpu

N_HEADS = 8
QK_NOPE_DIM = 64
QK_ROPE_DIM = 32
V_HEAD_DIM = 64
QK_DIM = QK_NOPE_DIM + QK_ROPE_DIM
Q_LORA_RANK = 384
KV_LORA_RANK = 256
F_GROUPS = 8
F_GROUP_DIM = 64
F_WIDTH = F_GROUPS * F_GROUP_DIM
N_EXPERTS = 32
TOP_K = 4
SWIGLU_LIMIT = 7.0
SWIGLU_ALPHA = 1.702
ROPE_THETA = 10000.0
RMS_EPS = 1e-6
N_MOD = 6

LANES = 128
SUBLANES = 8
VMEM_LIMIT_BYTES = 60 * 1024 * 1024

HEAD_PAD = LANES
KR_PAD = LANES
ROPE_HALF = QK_ROPE_DIM // 2
FFT_N2 = 64
MOE_ROWS = 256
ATTN_SUM_ROWS = 16
TOPK_PAD = 8
LOG2_E = math.log2(math.e)
NEG_BIG = -1e30

F32 = jnp.float32
BF16 = jnp.bfloat16


def _tiles(seq, n_tok):
    tm = min(512, seq // 2)
    tk = tm
    tq = min(512, seq)
    tc = min(4096, n_tok)
    return tm, tq, tk, tc


def _cparams(sem, flags=None):
    return pltpu.CompilerParams(dimension_semantics=sem, vmem_limit_bytes=VMEM_LIMIT_BYTES, flags=flags)


def _dot(a, b):
    return jnp.dot(a, b, preferred_element_type=F32)


def _dot_nt(a, b):
    return lax.dot_general(a, b, (((1,), (1,)), ((), ())), preferred_element_type=F32)


def _dot_tn(a, b):
    return lax.dot_general(a, b, (((0,), (0,)), ((), ())), preferred_element_type=F32)


def _split_bf16(x):
    hi = x.astype(BF16)
    lo = (x - hi.astype(F32)).astype(BF16)
    return hi, lo


def _rms(x, eps=RMS_EPS):
    return x * lax.rsqrt(jnp.mean(x * x, axis=-1, keepdims=True) + eps)


def _mod_kernel(c_ref, w_ref, b_ref, o_ref):
    c = c_ref[...]
    sc = c * jax.nn.sigmoid(c)
    c_hi, c_lo = _split_bf16(sc)
    w_hi, w_lo = _split_bf16(w_ref[...])
    o_ref[...] = _dot(c_hi, w_hi) + _dot(c_hi, w_lo) + _dot(c_lo, w_hi) + b_ref[...]


def _adaln_mod(c_all, w_ada, b_ada):
    n_layers, d, nd = w_ada.shape
    rows = c_all.shape[0]
    tn = min(nd, 1536)
    return pl.pallas_call(
        _mod_kernel,
        grid=(n_layers, nd // tn),
        in_specs=[
            pl.BlockSpec((rows, d), lambda l, j: (0, 0)),
            pl.BlockSpec((None, d, tn), lambda l, j: (l, 0, j)),
            pl.BlockSpec((None, 1, tn), lambda l, j: (l, 0, j)),
        ],
        out_specs=pl.BlockSpec((None, rows, tn), lambda l, j: (l, 0, j)),
        out_shape=jax.ShapeDtypeStruct((n_layers, rows, nd), F32),
        compiler_params=_cparams(("arbitrary", "arbitrary")),
        name="adaln_mod",
    )(c_all, w_ada, b_ada.reshape(n_layers, 1, nd))


def _unpack_rows(blk_ref, n_rows):
    nc = blk_ref.shape[0] // n_rows
    blk = blk_ref[...].reshape(n_rows // SUBLANES, nc, SUBLANES, LANES)
    return jnp.concatenate([blk[:, c].reshape(n_rows, LANES) for c in range(nc)], axis=1)


def _pack_rows(x):
    n_rows, width = x.shape
    nc = width // LANES
    parts = [x[:, c * LANES:(c + 1) * LANES].reshape(n_rows // SUBLANES, SUBLANES, LANES) for c in range(nc)]
    return jnp.stack(parts, axis=1).reshape(n_rows * nc, LANES)


def _rope(x, ctab, s1tab, s2tab):
    width = x.shape[1]
    return x * ctab + pltpu.roll(x, width - ROPE_HALF, 1) * s1tab + pltpu.roll(x, ROPE_HALF, 1) * s2tab


def _in_proj_kernel(*refs, d, has_moe):
    if has_moe:
        (x_ref, moe_ref, modp_ref, mod_ref, gmix_ref, win_ref, gq_ref, wuq_ref, gkv_ref, wkn_ref, wv_ref,
         ctab_ref, s1_ref, s2_ref, bdc_ref, bds_ref,
         q_ref, k_ref, vt_ref, w_ref, gates_ref, x2_ref) = refs
    else:
        (x_ref, mod_ref, gmix_ref, win_ref, gq_ref, wuq_ref, gkv_ref, wkn_ref, wv_ref,
         ctab_ref, s1_ref, s2_ref, bdc_ref, bds_ref,
         q_ref, k_ref, vt_ref, w_ref, gates_ref) = refs
    tm = x_ref.shape[0]
    x = x_ref[...]
    if has_moe:
        x = x + modp_ref[5:6, :] * _unpack_rows(moe_ref, tm)
        x2_ref[...] = x
    h = (_rms(x) * gmix_ref[...]) * (1.0 + mod_ref[1:2, :]) + mod_ref[0:1, :]
    hb = h.astype(BF16)

    o1 = Q_LORA_RANK
    o2 = o1 + KV_LORA_RANK
    o3 = o2 + KR_PAD
    o4 = o3 + F_WIDTH
    o5 = o4 + d

    ctab = jnp.concatenate([ctab_ref[...]] * N_HEADS, axis=1)
    s1tab = jnp.concatenate([s1_ref[...]] * N_HEADS, axis=1)
    s2tab = jnp.concatenate([s2_ref[...]] * N_HEADS, axis=1)

    uq = _dot(hb, win_ref[:, 0:o1])
    rq = (_rms(uq) * gq_ref[...]).astype(BF16)
    q = _rope(_dot(rq, wuq_ref[...]), ctab, s1tab, s2tab) * (QK_DIM ** -0.5 * LOG2_E)
    for hd in range(N_HEADS):
        q_ref[hd, :, :] = q[:, hd * HEAD_PAD:(hd + 1) * HEAD_PAD].astype(BF16)

    ukv = _dot(hb, win_ref[:, o1:o2])
    rkv = (_rms(ukv) * gkv_ref[...]).astype(BF16)
    ukr = _dot(hb, win_ref[:, o2:o3])
    kfull = _dot(rkv, wkn_ref[...]) + jnp.concatenate([ukr] * N_HEADS, axis=1)
    kk = _rope(kfull, ctab, s1tab, s2tab)
    for hd in range(N_HEADS):
        k_ref[hd, :, :] = kk[:, hd * HEAD_PAD:(hd + 1) * HEAD_PAD].astype(BF16)

    v = _dot(rkv, wv_ref[...])
    vt_ref[...] = v.T.astype(BF16)

    ufb = _dot(hb, win_ref[:, o3:o4]).astype(BF16)
    w_ref[0, :, :] = _dot(ufb, bdc_ref[...]).astype(BF16)
    w_ref[1, :, :] = (-_dot(ufb, bds_ref[...])).astype(BF16)

    gates_ref[:, 0:d] = jax.nn.sigmoid(_dot(hb, win_ref[:, o4:o5])).astype(BF16)
    gates_ref[:, d:2 * d] = jax.nn.sigmoid(_dot(hb, win_ref[:, o5:o5 + d])).astype(BF16)


def _in_proj(x, moe_prev, mod_prev, mod, gmix, win, gq, wuq, gkv, wkn, wv, tabs, bdc, bds, *, bsz, seq):
    n_tok, d = x.shape
    tm, _, _, _ = _tiles(seq, n_tok)
    has_moe = moe_prev is not None
    per_b = seq // tm
    nch = d // LANES
    win_w = win.shape[1]

    def tok(i):
        return (i, 0)

    def modmap(i):
        return (i // per_b, 0, 0)

    def const2(i):
        return (0, 0)

    def tabmap(i):
        return (i % per_b, 0)

    in_specs = [pl.BlockSpec((tm, d), tok)]
    args = [x]
    if has_moe:
        in_specs += [pl.BlockSpec((tm * nch, LANES), tok),
                     pl.BlockSpec((None, N_MOD, d), modmap)]
        args += [moe_prev, mod_prev]
    in_specs += [
        pl.BlockSpec((None, N_MOD, d), modmap),
        pl.BlockSpec((1, d), const2),
        pl.BlockSpec((d, win_w), const2),
        pl.BlockSpec((1, Q_LORA_RANK), const2),
        pl.BlockSpec((Q_LORA_RANK, N_HEADS * HEAD_PAD), const2),
        pl.BlockSpec((1, KV_LORA_RANK), const2),
        pl.BlockSpec((KV_LORA_RANK, N_HEADS * HEAD_PAD), const2),
        pl.BlockSpec((KV_LORA_RANK, N_HEADS * V_HEAD_DIM), const2),
        pl.BlockSpec((tm, HEAD_PAD), tabmap),
        pl.BlockSpec((tm, HEAD_PAD), tabmap),
        pl.BlockSpec((tm, HEAD_PAD), tabmap),
        pl.BlockSpec((F_WIDTH, F_WIDTH), const2),
        pl.BlockSpec((F_WIDTH, F_WIDTH), const2),
    ]
    args += [mod, gmix, win, gq, wuq, gkv, wkn, wv, tabs[0], tabs[1], tabs[2], bdc, bds]

    hv = N_HEADS * V_HEAD_DIM
    out_specs = [
        pl.BlockSpec((N_HEADS, tm, HEAD_PAD), lambda i: (0, i, 0)),
        pl.BlockSpec((N_HEADS, tm, HEAD_PAD), lambda i: (0, i, 0)),
        pl.BlockSpec((None, hv, tm), lambda i: (i, 0, 0)),
        pl.BlockSpec((2, tm, F_WIDTH), lambda i: (0, i, 0)),
        pl.BlockSpec((tm, 2 * d), tok),
    ]
    out_shape = [
        jax.ShapeDtypeStruct((N_HEADS, n_tok, HEAD_PAD), BF16),
        jax.ShapeDtypeStruct((N_HEADS, n_tok, HEAD_PAD), BF16),
        jax.ShapeDtypeStruct((n_tok // tm, hv, tm), BF16),
        jax.ShapeDtypeStruct((2, n_tok, F_WIDTH), BF16),
        jax.ShapeDtypeStruct((n_tok, 2 * d), BF16),
    ]
    if has_moe:
        out_specs.append(pl.BlockSpec((tm, d), tok))
        out_shape.append(jax.ShapeDtypeStruct((n_tok, d), F32))
    outs = pl.pallas_call(
        functools.partial(_in_proj_kernel, d=d, has_moe=has_moe),
        grid=(n_tok // tm,),
        in_specs=in_specs,
        out_specs=out_specs,
        out_shape=out_shape,
        compiler_params=_cparams(("arbitrary",)),
        name="in_proj",
    )(*args)
    if has_moe:
        return outs
    return (*outs, x)


def _attn_kernel(q_ref, k_ref, vt_ref, o_ref, sa_ref, sb_ref, *, tk, n_kv):
    q = q_ref[...]
    tq = q.shape[0]
    ones = jnp.ones((ATTN_SUM_ROWS, tk), BF16)

    def scores(j):
        return _dot_nt(k_ref[j * tk:(j + 1) * tk, :], q).astype(BF16)

    def update(j, s, m, acc):
        m_new = jnp.maximum(m, jnp.max(s, axis=0, keepdims=True).astype(F32))
        p = jnp.exp2(s - m_new.astype(BF16))
        alpha = jnp.exp2(m - m_new)
        per_tile = vt_ref.shape[2] // tk
        vj = vt_ref[j // per_tile, :, (j % per_tile) * tk:(j % per_tile + 1) * tk]
        v_ext = jnp.concatenate([vj, ones], axis=0)
        return m_new, alpha * acc + _dot(v_ext, p)

    bufs = (sa_ref, sb_ref)
    m = jnp.full((1, tq), NEG_BIG, F32)
    acc = jnp.zeros((V_HEAD_DIM + ATTN_SUM_ROWS, tq), F32)
    bufs[0][...] = scores(0)
    for j in range(n_kv):
        if j + 1 < n_kv:
            bufs[(j + 1) % 2][...] = scores(j + 1)
        m, acc = update(j, bufs[j % 2][...], m, acc)
    o_ref[...] = (acc[0:V_HEAD_DIM] / acc[V_HEAD_DIM:V_HEAD_DIM + 1]).astype(BF16)


def _attention(q, k, vt, *, bsz, seq):
    n_tok = q.shape[1]
    tm, tq, tk, _ = _tiles(seq, n_tok)
    n_q = seq // tq
    n_kv = seq // tk
    return pl.pallas_call(
        functools.partial(_attn_kernel, tk=tk, n_kv=n_kv),
        grid=(bsz, N_HEADS, n_q),
        in_specs=[
            pl.BlockSpec((None, tq, HEAD_PAD), lambda b, h, i: (h, b * n_q + i, 0)),
            pl.BlockSpec((None, seq, HEAD_PAD), lambda b, h, i: (h, b, 0)),
            pl.BlockSpec((seq // tm, V_HEAD_DIM, tm), lambda b, h, i: (b, h, 0)),
        ],
        out_specs=pl.BlockSpec((V_HEAD_DIM, tq), lambda b, h, i: (h, b * n_q + i)),
        out_shape=jax.ShapeDtypeStruct((N_HEADS * V_HEAD_DIM, n_tok), BF16),
        scratch_shapes=[pltpu.VMEM((tk, tq), BF16), pltpu.VMEM((tk, tq), BF16)],
        compiler_params=_cparams(("arbitrary", "arbitrary", "arbitrary")),
        name="attention",
    )(q, k, vt)


def _fft1_kernel(x_ref, f_ref, tr_ref, ti_ref, o_ref):
    n1 = tr_ref.shape[0]
    a = _dot(f_ref[...], x_ref[...])
    ar, ai = a[:n1], a[n1:]
    tr, ti = tr_ref[...], ti_ref[...]
    o_ref[0, :, :] = (ar * tr - ai * ti).astype(BF16)
    o_ref[1, :, :] = (ar * ti + ai * tr).astype(BF16)


def _fft2_kernel(x_ref, g_ref, o_ref):
    o_ref[...] = _dot(g_ref[...], x_ref[...]).astype(BF16)


def _seq_dft_real(w, consts, *, bsz, seq):
    f1, tr, ti, g2 = consts
    n2 = FFT_N2
    n1 = seq // n2
    fw = w.shape[2]
    lanes = n2 * fw
    lt = min(4096, lanes)
    xw = w.reshape(2, bsz, n1, lanes).transpose(1, 0, 2, 3).reshape(bsz, 2 * n1, lanes)
    b1 = pl.pallas_call(
        _fft1_kernel,
        grid=(bsz, lanes // lt),
        in_specs=[
            pl.BlockSpec((None, 2 * n1, lt), lambda b, j: (b, 0, j)),
            pl.BlockSpec((2 * n1, 2 * n1), lambda b, j: (0, 0)),
            pl.BlockSpec((n1, lt), lambda b, j: (0, j)),
            pl.BlockSpec((n1, lt), lambda b, j: (0, j)),
        ],
        out_specs=pl.BlockSpec((None, 2, n1, lt), lambda b, j: (b, 0, 0, j)),
        out_shape=jax.ShapeDtypeStruct((bsz, 2, n1, lanes), BF16),
        compiler_params=_cparams(("arbitrary", "arbitrary")),
        name="fft_stage1",
    )(xw, f1, tr, ti)
    lanes2 = n1 * fw
    lt2 = min(4096, lanes2)
    bt = b1.reshape(bsz, 2, n1, n2, fw).transpose(0, 1, 3, 2, 4).reshape(bsz, 2 * n2, lanes2)
    y = pl.pallas_call(
        _fft2_kernel,
        grid=(bsz, lanes2 // lt2),
        in_specs=[
            pl.BlockSpec((None, 2 * n2, lt2), lambda b, j: (b, 0, j)),
            pl.BlockSpec((n2, 2 * n2), lambda b, j: (0, 0)),
        ],
        out_specs=pl.BlockSpec((None, n2, lt2), lambda b, j: (b, 0, j)),
        out_shape=jax.ShapeDtypeStruct((bsz, n2, lanes2), BF16),
        compiler_params=_cparams(("arbitrary", "arbitrary")),
        name="fft_stage2",
    )(bt, g2)
    return y.reshape(bsz * seq, fw)


def _dft_consts(seq):
    n2 = FFT_N2
    n1 = seq // n2
    k1 = np.arange(n1)
    ang1 = 2.0 * np.pi * np.outer(k1, k1) / n1
    c1, s1 = np.cos(ang1), np.sin(ang1)
    f1 = np.block([[c1, s1], [-s1, c1]])
    k2 = np.arange(n2)
    ang2 = 2.0 * np.pi * np.outer(k2, k2) / n2
    scale = 1.0 / math.sqrt(seq * F_GROUP_DIM)
    g2 = np.concatenate([np.cos(ang2), np.sin(ang2)], axis=1) * scale
    prod = (jnp.arange(n1, dtype=jnp.int32)[:, None] * jnp.arange(n2, dtype=jnp.int32)[None, :]) % seq
    ang = prod.astype(F32) * (2.0 * math.pi / seq)
    tr = jnp.repeat(jnp.cos(ang), F_WIDTH, axis=1)
    ti = jnp.repeat(-jnp.sin(ang), F_WIDTH, axis=1)
    return jnp.asarray(f1, BF16), tr, ti, jnp.asarray(g2, BF16)


def _channel_dft_consts():
    j = np.arange(F_GROUP_DIM)
    ang = 2.0 * np.pi * np.outer(j, j) / F_GROUP_DIM
    eye = np.eye(F_GROUPS)
    return jnp.asarray(np.kron(eye, np.cos(ang)), BF16), jnp.asarray(np.kron(eye, np.sin(ang)), BF16)


def _rope_tabs(seq):
    inv = 1.0 / (ROPE_THETA ** (jnp.arange(0, QK_ROPE_DIM, 2, dtype=F32) / QK_ROPE_DIM))
    ang = jnp.arange(seq, dtype=F32)[:, None] * inv[None, :]
    cos, sin = jnp.cos(ang), jnp.sin(ang)
    ones = jnp.ones((seq, QK_NOPE_DIM), F32)
    zeros = jnp.zeros((seq, QK_NOPE_DIM), F32)
    pad1 = jnp.ones((seq, HEAD_PAD - QK_DIM), F32)
    pad0 = jnp.zeros((seq, HEAD_PAD - QK_DIM), F32)
    zr = jnp.zeros_like(sin)
    ctab = jnp.concatenate([ones, cos, cos, pad1], axis=1)
    s1tab = jnp.concatenate([zeros, -sin, zr, pad0], axis=1)
    s2tab = jnp.concatenate([zeros, zr, sin, pad0], axis=1)
    return ctab, s1tab, s2tab


def _post_kernel(ot_ref, yf_ref, gates_ref, x_ref, mod_ref, wa_ref, wb_ref, wout_ref, gffn_ref,
                 wrh_ref, wrl_ref, br_ref, x1_ref, h2p_ref, ti_ref, tg_ref, *, d):
    tm = x_ref.shape[0]
    ya = _dot_tn(ot_ref[...], wa_ref[...])
    yb = _dot(yf_ref[...], wb_ref[...])
    merged = gates_ref[:, 0:d].astype(F32) * ya + gates_ref[:, d:2 * d].astype(F32) * yb
    x1 = x_ref[...] + mod_ref[2:3, :] * _dot(merged.astype(BF16), wout_ref[...])
    x1_ref[...] = x1
    h2 = (_rms(x1) * gffn_ref[...]) * (1.0 + mod_ref[4:5, :]) + mod_ref[3:4, :]

    h_hi, h_lo = _split_bf16(h2)
    logits = _dot(h_hi, wrh_ref[...]) + _dot(h_hi, wrl_ref[...]) + _dot(h_lo, wrh_ref[...]) + br_ref[...]
    lane = lax.broadcasted_iota(jnp.int32, logits.shape, 1)
    work = logits
    vals, idxs = [], []
    for _ in range(TOP_K):
        mx = jnp.max(work, axis=1, keepdims=True)
        ix = jnp.min(jnp.where(work == mx, lane, LANES), axis=1, keepdims=True)
        vals.append(mx)
        idxs.append(ix)
        work = jnp.where(lane == ix, -jnp.inf, work)
    exps = [jnp.exp(v - vals[0]) for v in vals]
    tot = exps[0]
    for e in exps[1:]:
        tot = tot + e
    ti = jnp.zeros(logits.shape, jnp.int32)
    tg = jnp.zeros(logits.shape, F32)
    for kk in range(TOP_K):
        ti = jnp.where(lane == kk, idxs[kk], ti)
        tg = jnp.where(lane == kk, exps[kk] / tot, tg)
    ti_ref[...] = ti[:, 0:TOPK_PAD]
    tg_ref[...] = tg[:, 0:TOPK_PAD]

    bits = lax.bitcast_convert_type(h2.astype(BF16).astype(F32), jnp.uint32)
    half = d // 2
    h2p_ref[...] = _pack_rows((bits[:, 0:half] >> 16) | (bits[:, half:d] & jnp.uint32(0xFFFF0000)))


def _post_mix(ot, yf, gates, x, mod, wa, wb, wout, gffn, wrh, wrl, br, *, bsz, seq):
    n_tok, d = x.shape
    tm, _, _, _ = _tiles(seq, n_tok)
    per_b = seq // tm
    hv = N_HEADS * V_HEAD_DIM
    nchp = d // 2 // LANES

    def tok(i):
        return (i, 0)

    def const2(i):
        return (0, 0)

    return pl.pallas_call(
        functools.partial(_post_kernel, d=d),
        grid=(n_tok // tm,),
        in_specs=[
            pl.BlockSpec((hv, tm), lambda i: (0, i)),
            pl.BlockSpec((tm, F_WIDTH), tok),
            pl.BlockSpec((tm, 2 * d), tok),
            pl.BlockSpec((tm, d), tok),
            pl.BlockSpec((None, N_MOD, d), lambda i: (i // per_b, 0, 0)),
            pl.BlockSpec((hv, d), const2),
            pl.BlockSpec((F_WIDTH, d), const2),
            pl.BlockSpec((d, d), const2),
            pl.BlockSpec((1, d), const2),
            pl.BlockSpec((d, LANES), const2),
            pl.BlockSpec((d, LANES), const2),
            pl.BlockSpec((1, LANES), const2),
        ],
        out_specs=[
            pl.BlockSpec((tm, d), tok),
            pl.BlockSpec((tm * nchp, LANES), tok),
            pl.BlockSpec((tm, TOPK_PAD), tok),
            pl.BlockSpec((tm, TOPK_PAD), tok),
        ],
        out_shape=[
            jax.ShapeDtypeStruct((n_tok, d), F32),
            jax.ShapeDtypeStruct((n_tok * nchp, LANES), jnp.uint32),
            jax.ShapeDtypeStruct((n_tok, TOPK_PAD), jnp.int32),
            jax.ShapeDtypeStruct((n_tok, TOPK_PAD), F32),
        ],
        compiler_params=_cparams(("arbitrary",)),
        name="post_mix",
    )(ot, yf, gates, x, mod, wa, wb, wout, gffn, wrh, wrl, br)


def _moe_kernel(bexp_ref, nv_ref, tokg_ref, toks_ref, gates_ref, h2p_ref, wg_ref, wu_ref, wd_ref,
                bg_ref, bu_ref, bd_ref, out_ref, xa_ref, xb_ref, ya_ref, yb_ref, *, tc):
    c = pl.program_id(0)
    j = pl.program_id(1)
    nchp = h2p_ref.shape[0] // tc
    nc = out_ref.shape[0] // tc
    rows = MOE_ROWS
    gp = nchp * SUBLANES
    go = nc * SUBLANES

    @pl.when(j == 0)
    def _():
        out_ref[...] = jnp.zeros_like(out_ref)
        xb_ref[...] = jnp.zeros_like(xb_ref)
        ya_ref[...] = jnp.zeros_like(ya_ref)
        yb_ref[...] = jnp.zeros_like(yb_ref)

    def step(x_nxt, x_cur, y_cur, y_prv):
        for m in range(rows):
            t = tokg_ref[0, 0, m]
            x_nxt[pl.ds((m >> 3) * gp + (m & 7), nchp, stride=SUBLANES), :] = (
                h2p_ref[pl.ds((t >> 3) * gp + (t & 7), nchp, stride=SUBLANES), :])
        xs4 = x_cur[...].reshape(rows // SUBLANES, nchp, SUBLANES, LANES)
        los, his = [], []
        for cc in range(nchp):
            wd = xs4[:, cc].reshape(rows, LANES)
            los.append(lax.bitcast_convert_type(wd << 16, F32))
            his.append(lax.bitcast_convert_type(wd & jnp.uint32(0xFFFF0000), F32))
        x = jnp.concatenate(los + his, axis=1).astype(BF16)
        g = jnp.minimum(_dot(x, wg_ref[...]) + bg_ref[...], SWIGLU_LIMIT)
        u = jnp.clip(_dot(x, wu_ref[...]) + bu_ref[...], -SWIGLU_LIMIT, SWIGLU_LIMIT)
        act = ((u + 1.0) * (g * jax.nn.sigmoid(g * SWIGLU_ALPHA))).astype(BF16)
        y_cur[...] = _pack_rows(_dot(act, wd_ref[...]) + bd_ref[...])
        batch = 4
        for m0 in range(0, rows, batch):
            pend = []
            for m in range(m0, m0 + batch):
                t = toks_ref[0, 0, m]
                base = (t >> 3) * go + (t & 7)
                rv = y_prv[pl.ds((m >> 3) * go + (m & 7), nc, stride=SUBLANES), :]
                pend.append((base, out_ref[pl.ds(base, nc, stride=SUBLANES), :] + rv * gates_ref[0, 0, m]))
            for base, val in pend:
                out_ref[pl.ds(base, nc, stride=SUBLANES), :] = val

    live = j < nv_ref[c] + 2

    @pl.when(jnp.logical_and(live, j % 2 == 0))
    def _():
        step(xa_ref, xb_ref, yb_ref, ya_ref)

    @pl.when(jnp.logical_and(live, j % 2 == 1))
    def _():
        step(xb_ref, xa_ref, ya_ref, yb_ref)


def _moe_ffn(h2p, top_i, top_g, wg, wu, wd, bg, bu, bd, *, d):
    n_tok = top_i.shape[0]
    nchp = h2p.shape[0] // n_tok
    _, _, _, tc = _tiles(n_tok, n_tok)
    nc = d // LANES
    rows = MOE_ROWS
    n_chunks = n_tok // tc
    n_asg = tc * TOP_K
    n_blk = n_asg // rows + N_EXPERTS
    n_exp, _, ff = wg.shape

    e_flat = top_i.reshape(n_chunks, n_asg)
    g_flat = top_g.reshape(n_chunks, n_asg)
    key = e_flat * n_asg + jnp.arange(n_asg, dtype=jnp.int32)[None, :]
    key = jnp.sort(key, axis=1)
    order = key % n_asg
    counts = jnp.sum((e_flat[:, :, None] == jnp.arange(n_exp, dtype=jnp.int32)[None, None, :]).astype(jnp.int32),
                     axis=1)
    start = jnp.cumsum(counts, axis=1) - counts
    nb_e = (counts + rows - 1) // rows
    blk_end = jnp.cumsum(nb_e, axis=1)
    blk_start = blk_end - nb_e
    jb = jnp.arange(n_blk, dtype=jnp.int32)
    e_of_blk = jnp.sum((jb[None, :, None] >= blk_end[:, None, :]).astype(jnp.int32), axis=2)
    blk_valid = e_of_blk < n_exp
    last_e = jnp.max(jnp.where(counts > 0, jnp.arange(n_exp, dtype=jnp.int32)[None, :], 0), axis=1, keepdims=True)
    e_blk = jnp.where(blk_valid, jnp.minimum(e_of_blk, n_exp - 1), last_e)
    q0 = (jb[None, :] - jnp.take_along_axis(blk_start, e_blk, axis=1)) * rows
    cnt_blk = jnp.take_along_axis(counts, e_blk, axis=1)
    n_valid = jnp.where(blk_valid, jnp.clip(cnt_blk - q0, 0, rows), 0)
    r = jnp.arange(rows, dtype=jnp.int32)
    q = q0[:, :, None] + r[None, None, :]
    row_ok = r[None, None, :] < n_valid[:, :, None]
    s_idx = jnp.clip(jnp.take_along_axis(start, e_blk, axis=1)[:, :, None] + q, 0, n_asg - 1)
    a_idx = jnp.take_along_axis(order, s_idx.reshape(n_chunks, n_blk * rows), axis=1)
    row_tok = jnp.where(row_ok.reshape(n_chunks, -1), a_idx // TOP_K, 0).astype(jnp.int32)
    row_gate = jnp.where(row_ok.reshape(n_chunks, -1), jnp.take_along_axis(g_flat, a_idx, axis=1), 0.0)
    n_steps = n_blk + 2
    row_tok = row_tok.reshape(n_chunks, n_blk, rows)
    row_gate = row_gate.reshape(n_chunks, n_blk, rows).astype(F32)
    pad_i = jnp.zeros((n_chunks, 2, rows), jnp.int32)
    pad_f = jnp.zeros((n_chunks, 2, rows), F32)
    tok_gather = jnp.concatenate([row_tok, pad_i], axis=1).reshape(n_chunks * n_steps, 1, rows)
    tok_scatter = jnp.concatenate([pad_i, row_tok], axis=1).reshape(n_chunks * n_steps, 1, rows)
    gate_scatter = jnp.concatenate([pad_f, row_gate], axis=1).reshape(n_chunks * n_steps, 1, rows)
    e_blk = e_blk.astype(jnp.int32)
    bexp = jnp.concatenate([e_blk[:, :1], e_blk, e_blk[:, -1:]], axis=1).reshape(-1)
    n_live = blk_end[:, -1].astype(jnp.int32)

    def wmap(c, j, bexp_ref, nv_ref):
        return (bexp_ref[c * n_steps + j], 0, 0)

    def rowmap(c, j, bexp_ref, nv_ref):
        return (c * n_steps + j, 0, 0)

    def chunkmap(c, j, bexp_ref, nv_ref):
        return (c, 0)

    grid_spec = pltpu.PrefetchScalarGridSpec(
        num_scalar_prefetch=2,
        grid=(n_chunks, n_steps),
        in_specs=[
            pl.BlockSpec((1, 1, rows), rowmap, memory_space=pltpu.SMEM),
            pl.BlockSpec((1, 1, rows), rowmap, memory_space=pltpu.SMEM),
            pl.BlockSpec((1, 1, rows), rowmap, memory_space=pltpu.SMEM),
            pl.BlockSpec((tc * nchp, LANES), chunkmap, pipeline_mode=pl.Buffered(1)),
            pl.BlockSpec((None, d, ff), wmap),
            pl.BlockSpec((None, d, ff), wmap),
            pl.BlockSpec((None, ff, d), wmap),
            pl.BlockSpec((None, 1, ff), wmap),
            pl.BlockSpec((None, 1, ff), wmap),
            pl.BlockSpec((None, 1, d), wmap),
        ],
        out_specs=pl.BlockSpec((tc * nc, LANES), chunkmap),
        scratch_shapes=[
            pltpu.VMEM((rows * nchp, LANES), jnp.uint32),
            pltpu.VMEM((rows * nchp, LANES), jnp.uint32),
            pltpu.VMEM((rows * nc, LANES), F32),
            pltpu.VMEM((rows * nc, LANES), F32),
        ],
    )
    return pl.pallas_call(
        functools.partial(_moe_kernel, tc=tc),
        grid_spec=grid_spec,
        out_shape=jax.ShapeDtypeStruct((n_tok * nc, LANES), F32),
        compiler_params=_cparams(("arbitrary", "arbitrary")),
        name="moe_ffn",
    )(bexp, n_live, tok_gather, tok_scatter, gate_scatter, h2p, wg, wu, wd, bg, bu, bd)


def _final_kernel(x_ref, moe_ref, mod_ref, g_ref, o_ref):
    tm = x_ref.shape[0]
    x = x_ref[...] + mod_ref[5:6, :] * _unpack_rows(moe_ref, tm)
    o_ref[...] = _rms(x) * g_ref[...]


def _final_norm(x, moe, mod, g_final, *, bsz, seq):
    n_tok, d = x.shape
    tm, _, _, _ = _tiles(seq, n_tok)
    per_b = seq // tm
    nch = d // LANES
    return pl.pallas_call(
        _final_kernel,
        grid=(n_tok // tm,),
        in_specs=[
            pl.BlockSpec((tm, d), lambda i: (i, 0)),
            pl.BlockSpec((tm * nch, LANES), lambda i: (i, 0)),
            pl.BlockSpec((None, N_MOD, d), lambda i: (i // per_b, 0, 0)),
            pl.BlockSpec((1, d), lambda i: (0, 0)),
        ],
        out_specs=pl.BlockSpec((tm, d), lambda i: (i, 0)),
        out_shape=jax.ShapeDtypeStruct((n_tok, d), F32),
        compiler_params=_cparams(("arbitrary",)),
        name="final_norm",
    )(x, moe, mod, g_final)


def _deinterleave_kernel(w_ref, p_ref, g_ref, u_ref):
    for kg in range(w_ref.shape[1] // (2 * LANES)):
        blk = w_ref[:, kg * 2 * LANES:(kg + 1) * 2 * LANES].astype(BF16)
        sel = _dot(blk, p_ref[...])
        g_ref[:, kg * LANES:(kg + 1) * LANES] = sel[:, 0:LANES].astype(BF16)
        u_ref[:, kg * LANES:(kg + 1) * LANES] = sel[:, LANES:2 * LANES].astype(BF16)


def _deinterleave_gate_up(w_gu):
    n_layers, n_exp, d, ff2 = w_gu.shape
    rows = n_layers * n_exp * d
    tr = min(1024, rows)
    sel = np.zeros((2 * LANES, 2 * LANES), np.float32)
    sel[2 * np.arange(LANES), np.arange(LANES)] = 1.0
    sel[2 * np.arange(LANES) + 1, LANES + np.arange(LANES)] = 1.0
    g, u = pl.pallas_call(
        _deinterleave_kernel,
        grid=(rows // tr,),
        in_specs=[pl.BlockSpec((tr, ff2), lambda i: (i, 0)),
                  pl.BlockSpec((2 * LANES, 2 * LANES), lambda i: (0, 0))],
        out_specs=[pl.BlockSpec((tr, ff2 // 2), lambda i: (i, 0)),
                   pl.BlockSpec((tr, ff2 // 2), lambda i: (i, 0))],
        out_shape=[jax.ShapeDtypeStruct((rows, ff2 // 2), BF16),
                   jax.ShapeDtypeStruct((rows, ff2 // 2), BF16)],
        compiler_params=_cparams(("arbitrary",)),
        name="deinterleave_gate_up",
    )(w_gu.reshape(rows, ff2), jnp.asarray(sel, BF16))
    shape = (n_layers, n_exp, d, ff2 // 2)
    return g.reshape(shape), u.reshape(shape)


def _prep_weights(w_in, w_uq, w_ukv, w_a, w_b, w_out, w_router, b_router, w_gu, b_gu, w_dn, b_dn):
    n_layers, d, _ = w_in.shape
    o1 = Q_LORA_RANK
    o2 = o1 + KV_LORA_RANK
    o3 = o2 + QK_ROPE_DIM
    kr = jnp.zeros((n_layers, d, KR_PAD), w_in.dtype).at[:, :, QK_NOPE_DIM:QK_DIM].set(w_in[:, :, o2:o3])
    win = jnp.concatenate([w_in[:, :, :o2], kr, w_in[:, :, o3:]], axis=2).astype(BF16)
    wuq = jnp.pad(w_uq.reshape(n_layers, Q_LORA_RANK, N_HEADS, QK_DIM),
                  ((0, 0), (0, 0), (0, 0), (0, HEAD_PAD - QK_DIM)))
    wuq = wuq.reshape(n_layers, Q_LORA_RANK, N_HEADS * HEAD_PAD).astype(BF16)
    wkv = w_ukv.reshape(n_layers, KV_LORA_RANK, N_HEADS, QK_NOPE_DIM + V_HEAD_DIM)
    wkn = jnp.pad(wkv[..., :QK_NOPE_DIM], ((0, 0), (0, 0), (0, 0), (0, HEAD_PAD - QK_NOPE_DIM)))
    wkn = wkn.reshape(n_layers, KV_LORA_RANK, N_HEADS * HEAD_PAD).astype(BF16)
    wv = wkv[..., QK_NOPE_DIM:].reshape(n_layers, KV_LORA_RANK, N_HEADS * V_HEAD_DIM).astype(BF16)
    n_exp = w_router.shape[2]
    wr = jnp.pad(w_router, ((0, 0), (0, 0), (0, LANES - n_exp)))
    wr_hi = wr.astype(BF16)
    wr_lo = (wr - wr_hi.astype(F32)).astype(BF16)
    br = jnp.pad(b_router, ((0, 0), (0, LANES - n_exp)), constant_values=NEG_BIG)[:, None, :]
    wg, wu = _deinterleave_gate_up(w_gu)
    bg = b_gu[..., 0::2][:, :, None, :]
    bu = b_gu[..., 1::2][:, :, None, :]
    wd = w_dn.astype(BF16)
    bd = b_dn[:, :, None, :]
    return dict(win=win, wuq=wuq, wkn=wkn, wv=wv, wa=w_a.astype(BF16), wb=w_b.astype(BF16),
                wout=w_out.astype(BF16), wr_hi=wr_hi, wr_lo=wr_lo, br=br, wg=wg, wu=wu, bg=bg, bu=bu, wd=wd, bd=bd)


def _trunk(x3, mod, wts, g_mix, g_ffn, g_q, g_kv, g_final, bdc, bds):
    bsz, seq, d = x3.shape
    n_layers = g_mix.shape[0]
    x = x3.reshape(bsz * seq, d)
    tabs = _rope_tabs(seq)
    dft = _dft_consts(seq)
    moe = None
    for l in range(n_layers):
        mod_prev = mod[l - 1] if l > 0 else None
        q, k, vt, w, gates, x = _in_proj(
            x, moe, mod_prev, mod[l], g_mix[l][None, :], wts["win"][l], g_q[l][None, :], wts["wuq"][l],
            g_kv[l][None, :], wts["wkn"][l], wts["wv"][l], tabs, bdc, bds, bsz=bsz, seq=seq)
        ot = _attention(q, k, vt, bsz=bsz, seq=seq)
        yf = _seq_dft_real(w, dft, bsz=bsz, seq=seq)
        x, h2p, top_i, top_g = _post_mix(
            ot, yf, gates, x, mod[l], wts["wa"][l], wts["wb"][l], wts["wout"][l], g_ffn[l][None, :],
            wts["wr_hi"][l], wts["wr_lo"][l], wts["br"][l], bsz=bsz, seq=seq)
        moe = _moe_ffn(h2p, top_i[:, :TOP_K], top_g[:, :TOP_K], wts["wg"][l], wts["wu"][l], wts["wd"][l],
                       wts["bg"][l], wts["bu"][l], wts["bd"][l], d=d)
    y = _final_norm(x, moe, mod[n_layers - 1], g_final[None, :], bsz=bsz, seq=seq)
    return y.reshape(bsz, seq, d)


def kernel(x_prompt, x_sample, c_prompt, c_sample, w_ada, b_ada, g_mix, g_ffn, w_in, g_q, w_uq, g_kv, w_ukv, w_a, w_b, w_out, w_router, b_router, w_gu, b_gu, w_dn, b_dn, g_final):
    n_layers, d, _ = w_ada.shape
    bp, bs = c_prompt.shape[0], c_sample.shape[0]
    rows = -(-(bp + bs) // SUBLANES) * SUBLANES
    c_all = jnp.concatenate([c_prompt, c_sample, jnp.zeros((rows - bp - bs, d), c_prompt.dtype)], axis=0)
    mod = _adaln_mod(c_all, w_ada, b_ada).reshape(n_layers, rows, N_MOD, d)
    wts = _prep_weights(w_in, w_uq, w_ukv, w_a, w_b, w_out, w_router, b_router, w_gu, b_gu, w_dn, b_dn)
    bdc, bds = _channel_dft_consts()
    y_prompt = _trunk(x_prompt, mod[:, :bp], wts, g_mix, g_ffn, g_q, g_kv, g_final, bdc, bds)
    y_sample = _trunk(x_sample, mod[:, bp:bp + bs], wts, g_mix, g_ffn, g_q, g_kv, g_final, bdc, bds)
    return (y_prompt, y_sample)
```

```python
import functools
import math

import numpy as np
import jax
import jax.numpy as jnp
from jax import lax
from jax.experimental import pallas as pl
from jax.experimental.pallas import tpu as pltpu

N_HEADS = 8
QK_NOPE_DIM = 64
QK_ROPE_DIM = 32
V_HEAD_DIM = 64
QK_DIM = QK_NOPE_DIM + QK_ROPE_DIM
Q_LORA_RANK = 384
KV_LORA_RANK = 256
F_GROUPS = 8
F_GROUP_DIM = 64
F_WIDTH = F_GROUPS * F_GROUP_DIM
N_EXPERTS = 32
TOP_K = 4
SWIGLU_LIMIT = 7.0
SWIGLU_ALPHA = 1.702
ROPE_THETA = 10000.0
RMS_EPS = 1e-6
N_MOD = 6

LANES = 128
SUBLANES = 8
VMEM_LIMIT_BYTES = 60 * 1024 * 1024

HEAD_PAD = LANES
KR_PAD = LANES
ROPE_HALF = QK_ROPE_DIM // 2
FFT_N2 = 64
MOE_ROWS = 256
ATTN_SUM_ROWS = 16
ATTN_HEADS = 4
TOPK_PAD = 8
LOG2_E = math.log2(math.e)
NEG_BIG = -1e30

F32 = jnp.float32
BF16 = jnp.bfloat16


def _tiles(seq, n_tok):
    tm = min(512, seq // 2)
    tk = min(1024, seq // 2)
    tq = min(256, seq)
    tc = min(4096, n_tok)
    return tm, tq, tk, tc


def _cparams(sem, flags=None):
    return pltpu.CompilerParams(dimension_semantics=sem, vmem_limit_bytes=VMEM_LIMIT_BYTES, flags=flags)


def _dot(a, b):
    return jnp.dot(a, b, preferred_element_type=F32)


def _dot_nt(a, b):
    return lax.dot_general(a, b, (((1,), (1,)), ((), ())), preferred_element_type=F32)


def _dot_tn(a, b):
    return lax.dot_general(a, b, (((0,), (0,)), ((), ())), preferred_element_type=F32)


def _split_bf16(x):
    hi = x.astype(BF16)
    lo = (x - hi.astype(F32)).astype(BF16)
    return hi, lo


def _rms(x, eps=RMS_EPS):
    return x * lax.rsqrt(jnp.mean(x * x, axis=-1, keepdims=True) + eps)


def _mod_kernel(c_ref, w_ref, b_ref, o_ref):
    c = c_ref[...]
    sc = c * jax.nn.sigmoid(c)
    c_hi, c_lo = _split_bf16(sc)
    w_hi, w_lo = _split_bf16(w_ref[...])
    o_ref[...] = _dot(c_hi, w_hi) + _dot(c_hi, w_lo) + _dot(c_lo, w_hi) + b_ref[...]


def _adaln_mod(c_all, w_ada, b_ada):
    n_layers, d, nd = w_ada.shape
    rows = c_all.shape[0]
    tn = min(nd, 1536)
    return pl.pallas_call(
        _mod_kernel,
        grid=(n_layers, nd // tn),
        in_specs=[
            pl.BlockSpec((rows, d), lambda l, j: (0, 0)),
            pl.BlockSpec((None, d, tn), lambda l, j: (l, 0, j)),
            pl.BlockSpec((None, 1, tn), lambda l, j: (l, 0, j)),
        ],
        out_specs=pl.BlockSpec((None, rows, tn), lambda l, j: (l, 0, j)),
        out_shape=jax.ShapeDtypeStruct((n_layers, rows, nd), F32),
        compiler_params=_cparams(("arbitrary", "arbitrary")),
        name="adaln_mod",
    )(c_all, w_ada, b_ada.reshape(n_layers, 1, nd))


def _unpack_rows(blk_ref, n_rows):
    nc = blk_ref.shape[0] // n_rows
    blk = blk_ref[...].reshape(n_rows // SUBLANES, nc, SUBLANES, LANES)
    return jnp.concatenate([blk[:, c].reshape(n_rows, LANES) for c in range(nc)], axis=1)


def _pack_rows(x):
    n_rows, width = x.shape
    nc = width // LANES
    parts = [x[:, c * LANES:(c + 1) * LANES].reshape(n_rows // SUBLANES, SUBLANES, LANES) for c in range(nc)]
    return jnp.stack(parts, axis=1).reshape(n_rows * nc, LANES)


def _rope(x, ctab, s1tab, s2tab):
    width = x.shape[1]
    return x * ctab + pltpu.roll(x, width - ROPE_HALF, 1) * s1tab + pltpu.roll(x, ROPE_HALF, 1) * s2tab


def _in_proj_kernel(*refs, d, has_moe):
    if has_moe:
        (x_ref, moe_ref, modp_ref, mod_ref, gmix_ref, win_ref, gq_ref, wuq_ref, gkv_ref, wkn_ref, wv_ref,
         ctab_ref, s1_ref, s2_ref, bdc_ref, bds_ref,
         q_ref, k_ref, vt_ref, w_ref, gates_ref, x2_ref) = refs
    else:
        (x_ref, mod_ref, gmix_ref, win_ref, gq_ref, wuq_ref, gkv_ref, wkn_ref, wv_ref,
         ctab_ref, s1_ref, s2_ref, bdc_ref, bds_ref,
         q_ref, k_ref, vt_ref, w_ref, gates_ref) = refs
    tm = x_ref.shape[0]
    x = x_ref[...]
    if has_moe:
        x = x + modp_ref[5:6, :] * _unpack_rows(moe_ref, tm)
        x2_ref[...] = x
    h = (_rms(x) * gmix_ref[...]) * (1.0 + mod_ref[1:2, :]) + mod_ref[0:1, :]
    hb = h.astype(BF16)

    o1 = Q_LORA_RANK
    o2 = o1 + KV_LORA_RANK
    o3 = o2 + KR_PAD
    o4 = o3 + F_WIDTH
    o5 = o4 + d

    ctab = jnp.concatenate([ctab_ref[...]] * N_HEADS, axis=1)
    s1tab = jnp.concatenate([s1_ref[...]] * N_HEADS, axis=1)
    s2tab = jnp.concatenate([s2_ref[...]] * N_HEADS, axis=1)

    uq = _dot(hb, win_ref[:, 0:o1])
    rq = (_rms(uq) * gq_ref[...]).astype(BF16)
    q = _rope(_dot(rq, wuq_ref[...]), ctab, s1tab, s2tab) * (QK_DIM ** -0.5 * LOG2_E)
    for hd in range(N_HEADS):
        q_ref[hd, :, :] = q[:, hd * HEAD_PAD:(hd + 1) * HEAD_PAD].astype(BF16)

    ukv = _dot(hb, win_ref[:, o1:o2])
    rkv = (_rms(ukv) * gkv_ref[...]).astype(BF16)
    ukr = _dot(hb, win_ref[:, o2:o3])
    kfull = _dot(rkv, wkn_ref[...]) + jnp.concatenate([ukr] * N_HEADS, axis=1)
    kk = _rope(kfull, ctab, s1tab, s2tab)
    for hd in range(N_HEADS):
        k_ref[hd, :, :] = kk[:, hd * HEAD_PAD:(hd + 1) * HEAD_PAD].astype(BF16)

    v = _dot(rkv, wv_ref[...])
    vt_ref[...] = v.T.astype(BF16)

    ufb = _dot(hb, win_ref[:, o3:o4]).astype(BF16)
    w_ref[0, :, :] = _dot(ufb, bdc_ref[...]).astype(BF16)
    w_ref[1, :, :] = (-_dot(ufb, bds_ref[...])).astype(BF16)

    gates_ref[:, 0:d] = jax.nn.sigmoid(_dot(hb, win_ref[:, o4:o5])).astype(BF16)
    gates_ref[:, d:2 * d] = jax.nn.sigmoid(_dot(hb, win_ref[:, o5:o5 + d])).astype(BF16)


def _in_proj(x, moe_prev, mod_prev, mod, gmix, win, gq, wuq, gkv, wkn, wv, tabs, bdc, bds, *, bsz, seq):
    n_tok, d = x.shape
    tm, _, _, _ = _tiles(seq, n_tok)
    has_moe = moe_prev is not None
    per_b = seq // tm
    nch = d // LANES
    win_w = win.shape[1]

    def tok(i):
        return (i, 0)

    def modmap(i):
        return (i // per_b, 0, 0)

    def const2(i):
        return (0, 0)

    def tabmap(i):
        return (i % per_b, 0)

    in_specs = [pl.BlockSpec((tm, d), tok)]
    args = [x]
    if has_moe:
        in_specs += [pl.BlockSpec((tm * nch, LANES), tok),
                     pl.BlockSpec((None, N_MOD, d), modmap)]
        args += [moe_prev, mod_prev]
    in_specs += [
        pl.BlockSpec((None, N_MOD, d), modmap),
        pl.BlockSpec((1, d), const2),
        pl.BlockSpec((d, win_w), const2),
        pl.BlockSpec((1, Q_LORA_RANK), const2),
        pl.BlockSpec((Q_LORA_RANK, N_HEADS * HEAD_PAD), const2),
        pl.BlockSpec((1, KV_LORA_RANK), const2),
        pl.BlockSpec((KV_LORA_RANK, N_HEADS * HEAD_PAD), const2),
        pl.BlockSpec((KV_LORA_RANK, N_HEADS * V_HEAD_DIM), const2),
        pl.BlockSpec((tm, HEAD_PAD), tabmap),
        pl.BlockSpec((tm, HEAD_PAD), tabmap),
        pl.BlockSpec((tm, HEAD_PAD), tabmap),
        pl.BlockSpec((F_WIDTH, F_WIDTH), const2),
        pl.BlockSpec((F_WIDTH, F_WIDTH), const2),
    ]
    args += [mod, gmix, win, gq, wuq, gkv, wkn, wv, tabs[0], tabs[1], tabs[2], bdc, bds]

    hv = N_HEADS * V_HEAD_DIM
    out_specs = [
        pl.BlockSpec((N_HEADS, tm, HEAD_PAD), lambda i: (0, i, 0)),
        pl.BlockSpec((N_HEADS, tm, HEAD_PAD), lambda i: (0, i, 0)),
        pl.BlockSpec((None, hv, tm), lambda i: (i, 0, 0)),
        pl.BlockSpec((2, tm, F_WIDTH), lambda i: (0, i, 0)),
        pl.BlockSpec((tm, 2 * d), tok),
    ]
    out_shape = [
        jax.ShapeDtypeStruct((N_HEADS, n_tok, HEAD_PAD), BF16),
        jax.ShapeDtypeStruct((N_HEADS, n_tok, HEAD_PAD), BF16),
        jax.ShapeDtypeStruct((n_tok // tm, hv, tm), BF16),
        jax.ShapeDtypeStruct((2, n_tok, F_WIDTH), BF16),
        jax.ShapeDtypeStruct((n_tok, 2 * d), BF16),
    ]
    if has_moe:
        out_specs.append(pl.BlockSpec((tm, d), tok))
        out_shape.append(jax.ShapeDtypeStruct((n_tok, d), F32))
    outs = pl.pallas_call(
        functools.partial(_in_proj_kernel, d=d, has_moe=has_moe),
        grid=(n_tok // tm,),
        in_specs=in_specs,
        out_specs=out_specs,
        out_shape=out_shape,
        compiler_params=_cparams(("arbitrary",)),
        name="in_proj",
    )(*args)
    if has_moe:
        return outs
    return (*outs, x)


def _attn_kernel(q_ref, k_ref, vt_ref, o_ref, *bufs, tk, n_kv):
    n_hd, tq, _ = q_ref.shape
    ones = jnp.ones((ATTN_SUM_ROWS, tk), BF16)
    per_tile = vt_ref.shape[2] // tk

    def scores(hd, j):
        return _dot_nt(k_ref[hd, j * tk:(j + 1) * tk, :], q_ref[hd]).astype(BF16)

    def update(hd, j, s, m, acc):
        m_new = jnp.maximum(m, jnp.max(s, axis=0, keepdims=True).astype(F32))
        p = jnp.exp2(s - m_new.astype(BF16))
        alpha = jnp.exp2(m - m_new)
        rows = slice(hd * V_HEAD_DIM, (hd + 1) * V_HEAD_DIM)
        if per_tile >= 1:
            vj = vt_ref[j // per_tile, rows, (j % per_tile) * tk:(j % per_tile + 1) * tk]
        else:
            n_t = tk // vt_ref.shape[2]
            vj = jnp.concatenate([vt_ref[j * n_t + t, rows, :] for t in range(n_t)], axis=1)
        v_ext = jnp.concatenate([vj, ones], axis=0)
        return m_new, alpha * acc + _dot(v_ext, p)

    m = [jnp.full((1, tq), NEG_BIG, F32) for _ in range(n_hd)]
    acc = [jnp.zeros((V_HEAD_DIM + ATTN_SUM_ROWS, tq), F32) for _ in range(n_hd)]
    for hd in range(n_hd):
        bufs[2 * hd][...] = scores(hd, 0)
    for j in range(n_kv):
        for hd in range(n_hd):
            if j + 1 < n_kv:
                bufs[2 * hd + (j + 1) % 2][...] = scores(hd, j + 1)
            m[hd], acc[hd] = update(hd, j, bufs[2 * hd + j % 2][...], m[hd], acc[hd])
    for hd in range(n_hd):
        o_ref[hd * V_HEAD_DIM:(hd + 1) * V_HEAD_DIM, :] = (
            acc[hd][0:V_HEAD_DIM] / acc[hd][V_HEAD_DIM:V_HEAD_DIM + 1]).astype(BF16)


def _attention(q, k, vt, *, bsz, seq):
    n_tok = q.shape[1]
    tm, tq, tk, _ = _tiles(seq, n_tok)
    n_q = seq // tq
    n_kv = seq // tk
    n_hd = ATTN_HEADS
    return pl.pallas_call(
        functools.partial(_attn_kernel, tk=tk, n_kv=n_kv),
        grid=(bsz, N_HEADS // n_hd, n_q),
        in_specs=[
            pl.BlockSpec((n_hd, tq, HEAD_PAD), lambda b, h, i: (h, b * n_q + i, 0)),
            pl.BlockSpec((n_hd, seq, HEAD_PAD), lambda b, h, i: (h, b, 0)),
            pl.BlockSpec((seq // tm, n_hd * V_HEAD_DIM, tm), lambda b, h, i: (b, h, 0)),
        ],
        out_specs=pl.BlockSpec((n_hd * V_HEAD_DIM, tq), lambda b, h, i: (h, b * n_q + i)),
        out_shape=jax.ShapeDtypeStruct((N_HEADS * V_HEAD_DIM, n_tok), BF16),
        scratch_shapes=[pltpu.VMEM((tk, tq), BF16) for _ in range(2 * n_hd)],
        compiler_params=_cparams(("arbitrary", "arbitrary", "arbitrary")),
        name="attention",
    )(q, k, vt)


def _fft1_kernel(x_ref, f_ref, tr_ref, ti_ref, o_ref):
    n1 = tr_ref.shape[0]
    a = _dot(f_ref[...], x_ref[...])
    ar, ai = a[:n1], a[n1:]
    tr, ti = tr_ref[...], ti_ref[...]
    o_ref[0, :, :] = (ar * tr - ai * ti).astype(BF16)
    o_ref[1, :, :] = (ar * ti + ai * tr).astype(BF16)


def _fft2_kernel(x_ref, g_ref, o_ref):
    o_ref[...] = _dot(g_ref[...], x_ref[...]).astype(BF16)


def _seq_dft_real(w, consts, *, bsz, seq):
    f1, tr, ti, g2 = consts
    n2 = FFT_N2
    n1 = seq // n2
    fw = w.shape[2]
    lanes = n2 * fw
    lt = min(4096, lanes)
    xw = w.reshape(2, bsz, n1, lanes).transpose(1, 0, 2, 3).reshape(bsz, 2 * n1, lanes)
    b1 = pl.pallas_call(
        _fft1_kernel,
        grid=(bsz, lanes // lt),
        in_specs=[
            pl.BlockSpec((None, 2 * n1, lt), lambda b, j: (b, 0, j)),
            pl.BlockSpec((2 * n1, 2 * n1), lambda b, j: (0, 0)),
            pl.BlockSpec((n1, lt), lambda b, j: (0, j)),
            pl.BlockSpec((n1, lt), lambda b, j: (0, j)),
        ],
        out_specs=pl.BlockSpec((None, 2, n1, lt), lambda b, j: (b, 0, 0, j)),
        out_shape=jax.ShapeDtypeStruct((bsz, 2, n1, lanes), BF16),
        compiler_params=_cparams(("arbitrary", "arbitrary")),
        name="fft_stage1",
    )(xw, f1, tr, ti)
    lanes2 = n1 * fw
    lt2 = min(4096, lanes2)
    bt = b1.reshape(bsz, 2, n1, n2, fw).transpose(0, 1, 3, 2, 4).reshape(bsz, 2 * n2, lanes2)
    y = pl.pallas_call(
        _fft2_kernel,
        grid=(bsz, lanes2 // lt2),
        in_specs=[
            pl.BlockSpec((None, 2 * n2, lt2), lambda b, j: (b, 0, j)),
            pl.BlockSpec((n2, 2 * n2), lambda b, j: (0, 0)),
        ],
        out_specs=pl.BlockSpec((None, n2, lt2), lambda b, j: (b, 0, j)),
        out_shape=jax.ShapeDtypeStruct((bsz, n2, lanes2), BF16),
        compiler_params=_cparams(("arbitrary", "arbitrary")),
        name="fft_stage2",
    )(bt, g2)
    return y.reshape(bsz * seq, fw)


def _dft_consts(seq):
    n2 = FFT_N2
    n1 = seq // n2
    k1 = np.arange(n1)
    ang1 = 2.0 * np.pi * np.outer(k1, k1) / n1
    c1, s1 = np.cos(ang1), np.sin(ang1)
    f1 = np.block([[c1, s1], [-s1, c1]])
    k2 = np.arange(n2)
    ang2 = 2.0 * np.pi * np.outer(k2, k2) / n2
    scale = 1.0 / math.sqrt(seq * F_GROUP_DIM)
    g2 = np.concatenate([np.cos(ang2), np.sin(ang2)], axis=1) * scale
    prod = (jnp.arange(n1, dtype=jnp.int32)[:, None] * jnp.arange(n2, dtype=jnp.int32)[None, :]) % seq
    ang = prod.astype(F32) * (2.0 * math.pi / seq)
    tr = jnp.repeat(jnp.cos(ang), F_WIDTH, axis=1)
    ti = jnp.repeat(-jnp.sin(ang), F_WIDTH, axis=1)
    return jnp.asarray(f1, BF16), tr, ti, jnp.asarray(g2, BF16)


def _channel_dft_consts():
    j = np.arange(F_GROUP_DIM)
    ang = 2.0 * np.pi * np.outer(j, j) / F_GROUP_DIM
    eye = np.eye(F_GROUPS)
    return jnp.asarray(np.kron(eye, np.cos(ang)), BF16), jnp.asarray(np.kron(eye, np.sin(ang)), BF16)


def _rope_tabs(seq):
    inv = 1.0 / (ROPE_THETA ** (jnp.arange(0, QK_ROPE_DIM, 2, dtype=F32) / QK_ROPE_DIM))
    ang = jnp.arange(seq, dtype=F32)[:, None] * inv[None, :]
    cos, sin = jnp.cos(ang), jnp.sin(ang)
    ones = jnp.ones((seq, QK_NOPE_DIM), F32)
    zeros = jnp.zeros((seq, QK_NOPE_DIM), F32)
    pad1 = jnp.ones((seq, HEAD_PAD - QK_DIM), F32)
    pad0 = jnp.zeros((seq, HEAD_PAD - QK_DIM), F32)
    zr = jnp.zeros_like(sin)
    ctab = jnp.concatenate([ones, cos, cos, pad1], axis=1)
    s1tab = jnp.concatenate([zeros, -sin, zr, pad0], axis=1)
    s2tab = jnp.concatenate([zeros, zr, sin, pad0], axis=1)
    return ctab, s1tab, s2tab


def _post_kernel(ot_ref, yf_ref, gates_ref, x_ref, mod_ref, wa_ref, wb_ref, wout_ref, gffn_ref,
                 wrh_ref, wrl_ref, br_ref, x1_ref, h2p_ref, ti_ref, tg_ref, *, d):
    tm = x_ref.shape[0]
    ya = _dot_tn(ot_ref[...], wa_ref[...])
    yb = _dot(yf_ref[...], wb_ref[...])
    merged = gates_ref[:, 0:d].astype(F32) * ya + gates_ref[:, d:2 * d].astype(F32) * yb
    x1 = x_ref[...] + mod_ref[2:3, :] * _dot(merged.astype(BF16), wout_ref[...])
    x1_ref[...] = x1
    h2 = (_rms(x1) * gffn_ref[...]) * (1.0 + mod_ref[4:5, :]) + mod_ref[3:4, :]

    h_hi, h_lo = _split_bf16(h2)
    logits = _dot(h_hi, wrh_ref[...]) + _dot(h_hi, wrl_ref[...]) + _dot(h_lo, wrh_ref[...]) + br_ref[...]
    lane = lax.broadcasted_iota(jnp.int32, logits.shape, 1)
    work = logits
    vals, idxs = [], []
    for _ in range(TOP_K):
        mx = jnp.max(work, axis=1, keepdims=True)
        ix = jnp.min(jnp.where(work == mx, lane, LANES), axis=1, keepdims=True)
        vals.append(mx)
        idxs.append(ix)
        work = jnp.where(lane == ix, -jnp.inf, work)
    exps = [jnp.exp(v - vals[0]) for v in vals]
    tot = exps[0]
    for e in exps[1:]:
        tot = tot + e
    ti = jnp.zeros(logits.shape, jnp.int32)
    tg = jnp.zeros(logits.shape, F32)
    for kk in range(TOP_K):
        ti = jnp.where(lane == kk, idxs[kk], ti)
        tg = jnp.where(lane == kk, exps[kk] / tot, tg)
    ti_ref[...] = ti[:, 0:TOPK_PAD]
    tg_ref[...] = tg[:, 0:TOPK_PAD]

    bits = lax.bitcast_convert_type(h2.astype(BF16).astype(F32), jnp.uint32)
    half = d // 2
    h2p_ref[...] = _pack_rows((bits[:, 0:half] >> 16) | (bits[:, half:d] & jnp.uint32(0xFFFF0000)))


def _post_mix(ot, yf, gates, x, mod, wa, wb, wout, gffn, wrh, wrl, br, *, bsz, seq):
    n_tok, d = x.shape
    tm, _, _, _ = _tiles(seq, n_tok)
    per_b = seq // tm
    hv = N_HEADS * V_HEAD_DIM
    nchp = d // 2 // LANES

    def tok(i):
        return (i, 0)

    def const2(i):
        return (0, 0)

    return pl.pallas_call(
        functools.partial(_post_kernel, d=d),
        grid=(n_tok // tm,),
        in_specs=[
            pl.BlockSpec((hv, tm), lambda i: (0, i)),
            pl.BlockSpec((tm, F_WIDTH), tok),
            pl.BlockSpec((tm, 2 * d), tok),
            pl.BlockSpec((tm, d), tok),
            pl.BlockSpec((None, N_MOD, d), lambda i: (i // per_b, 0, 0)),
            pl.BlockSpec((hv, d), const2),
            pl.BlockSpec((F_WIDTH, d), const2),
            pl.BlockSpec((d, d), const2),
            pl.BlockSpec((1, d), const2),
            pl.BlockSpec((d, LANES), const2),
            pl.BlockSpec((d, LANES), const2),
            pl.BlockSpec((1, LANES), const2),
        ],
        out_specs=[
            pl.BlockSpec((tm, d), tok),
            pl.BlockSpec((tm * nchp, LANES), tok),
            pl.BlockSpec((tm, TOPK_PAD), tok),
            pl.BlockSpec((tm, TOPK_PAD), tok),
        ],
        out_shape=[
            jax.ShapeDtypeStruct((n_tok, d), F32),
            jax.ShapeDtypeStruct((n_tok * nchp, LANES), jnp.uint32),
            jax.ShapeDtypeStruct((n_tok, TOPK_PAD), jnp.int32),
            jax.ShapeDtypeStruct((n_tok, TOPK_PAD), F32),
        ],
        compiler_params=_cparams(("arbitrary",)),
        name="post_mix",
    )(ot, yf, gates, x, mod, wa, wb, wout, gffn, wrh, wrl, br)


def _moe_kernel(bexp_ref, nv_ref, tab_ref, h2p_ref, wg_ref, wu_ref, wd_ref,
                bg_ref, bu_ref, bd_ref, out_ref, xa_ref, xb_ref, ya_ref, yb_ref, *, tc):
    c = pl.program_id(0)
    j = pl.program_id(1)
    nchp = h2p_ref.shape[0] // tc
    nc = out_ref.shape[0] // tc
    rows = MOE_ROWS
    gp = nchp * SUBLANES
    go = nc * SUBLANES

    @pl.when(j == 0)
    def _():
        out_ref[...] = jnp.zeros_like(out_ref)
        xb_ref[...] = jnp.zeros_like(xb_ref)
        ya_ref[...] = jnp.zeros_like(ya_ref)
        yb_ref[...] = jnp.zeros_like(yb_ref)

    def step(x_nxt, x_cur, y_cur, y_prv):
        for m in range(rows):
            t = tab_ref[0, 0, m]
            x_nxt[pl.ds((m >> 3) * gp + (m & 7), nchp, stride=SUBLANES), :] = (
                h2p_ref[pl.ds((t >> 3) * gp + (t & 7), nchp, stride=SUBLANES), :])
        xs4 = x_cur[...].reshape(rows // SUBLANES, nchp, SUBLANES, LANES)
        los, his = [], []
        for cc in range(nchp):
            wd = xs4[:, cc].reshape(rows, LANES)
            los.append(lax.bitcast_convert_type(wd << 16, F32))
            his.append(lax.bitcast_convert_type(wd & jnp.uint32(0xFFFF0000), F32))
        x = jnp.concatenate(los + his, axis=1).astype(BF16)
        g = jnp.minimum(_dot(x, wg_ref[...]) + bg_ref[...], SWIGLU_LIMIT)
        u = jnp.clip(_dot(x, wu_ref[...]) + bu_ref[...], -SWIGLU_LIMIT, SWIGLU_LIMIT)
        act = ((u + 1.0) * (g * jax.nn.sigmoid(g * SWIGLU_ALPHA))).astype(BF16)
        y_cur[...] = _pack_rows(_dot(act, wd_ref[...]) + bd_ref[...])
        batch = 4
        for m0 in range(0, rows, batch):
            pend = []
            for m in range(m0, m0 + batch):
                t = tab_ref[0, 1, m]
                gate = lax.bitcast_convert_type(tab_ref[0, 2, m], F32)
                base = (t >> 3) * go + (t & 7)
                rv = y_prv[pl.ds((m >> 3) * go + (m & 7), nc, stride=SUBLANES), :]
                pend.append((base, out_ref[pl.ds(base, nc, stride=SUBLANES), :] + rv * gate))
            for base, val in pend:
                out_ref[pl.ds(base, nc, stride=SUBLANES), :] = val

    live = j < nv_ref[c] + 2

    @pl.when(jnp.logical_and(live, j % 2 == 0))
    def _():
        step(xa_ref, xb_ref, yb_ref, ya_ref)

    @pl.when(jnp.logical_and(live, j % 2 == 1))
    def _():
        step(xb_ref, xa_ref, ya_ref, yb_ref)


def _moe_ffn(h2p, top_i, top_g, wg, wu, wd, bg, bu, bd, *, d):
    n_tok = top_i.shape[0]
    nchp = h2p.shape[0] // n_tok
    _, _, _, tc = _tiles(n_tok, n_tok)
    nc = d // LANES
    rows = MOE_ROWS
    n_chunks = n_tok // tc
    n_asg = tc * TOP_K
    n_blk = n_asg // rows + N_EXPERTS
    n_exp, _, ff = wg.shape

    e_flat = top_i.reshape(n_chunks, n_asg)
    g_flat = top_g.reshape(n_chunks, n_asg)
    key = e_flat * n_asg + jnp.arange(n_asg, dtype=jnp.int32)[None, :]
    key = jnp.sort(key, axis=1)
    order = key % n_asg
    counts = jnp.sum((e_flat[:, :, None] == jnp.arange(n_exp, dtype=jnp.int32)[None, None, :]).astype(jnp.int32),
                     axis=1)
    start = jnp.cumsum(counts, axis=1) - counts
    nb_e = (counts + rows - 1) // rows
    blk_end = jnp.cumsum(nb_e, axis=1)
    blk_start = blk_end - nb_e
    jb = jnp.arange(n_blk, dtype=jnp.int32)
    e_of_blk = jnp.sum((jb[None, :, None] >= blk_end[:, None, :]).astype(jnp.int32), axis=2)
    blk_valid = e_of_blk < n_exp
    last_e = jnp.max(jnp.where(counts > 0, jnp.arange(n_exp, dtype=jnp.int32)[None, :], 0), axis=1, keepdims=True)
    e_blk = jnp.where(blk_valid, jnp.minimum(e_of_blk, n_exp - 1), last_e)
    q0 = (jb[None, :] - jnp.take_along_axis(blk_start, e_blk, axis=1)) * rows
    cnt_blk = jnp.take_along_axis(counts, e_blk, axis=1)
    n_valid = jnp.where(blk_valid, jnp.clip(cnt_blk - q0, 0, rows), 0)
    r = jnp.arange(rows, dtype=jnp.int32)
    q = q0[:, :, None] + r[None, None, :]
    row_ok = r[None, None, :] < n_valid[:, :, None]
    s_idx = jnp.clip(jnp.take_along_axis(start, e_blk, axis=1)[:, :, None] + q, 0, n_asg - 1)
    a_idx = jnp.take_along_axis(order, s_idx.reshape(n_chunks, n_blk * rows), axis=1)
    row_tok = jnp.where(row_ok.reshape(n_chunks, -1), a_idx // TOP_K, 0).astype(jnp.int32)
    row_gate = jnp.where(row_ok.reshape(n_chunks, -1), jnp.take_along_axis(g_flat, a_idx, axis=1), 0.0)
    n_steps = n_blk + 2
    row_tok = row_tok.reshape(n_chunks, n_blk, rows)
    row_gate = row_gate.reshape(n_chunks, n_blk, rows).astype(F32)
    pad_i = jnp.zeros((n_chunks, 2, rows), jnp.int32)
    gate_bits = lax.bitcast_convert_type(row_gate, jnp.int32)
    table = jnp.stack([jnp.concatenate([row_tok, pad_i], axis=1),
                       jnp.concatenate([pad_i, row_tok], axis=1),
                       jnp.concatenate([pad_i, gate_bits], axis=1)], axis=2)
    table = table.reshape(n_chunks * n_steps, 3, rows)
    e_blk = e_blk.astype(jnp.int32)
    bexp = jnp.concatenate([e_blk[:, :1], e_blk, e_blk[:, -1:]], axis=1).reshape(-1)
    n_live = blk_end[:, -1].astype(jnp.int32)

    def wmap(c, j, bexp_ref, nv_ref):
        return (bexp_ref[c * n_steps + j], 0, 0)

    def rowmap(c, j, bexp_ref, nv_ref):
        return (c * n_steps + j, 0, 0)

    def chunkmap(c, j, bexp_ref, nv_ref):
        return (c, 0)

    grid_spec = pltpu.PrefetchScalarGridSpec(
        num_scalar_prefetch=2,
        grid=(n_chunks, n_steps),
        in_specs=[
            pl.BlockSpec((1, 3, rows), rowmap, memory_space=pltpu.SMEM),
            pl.BlockSpec((tc * nchp, LANES), chunkmap, pipeline_mode=pl.Buffered(1)),
            pl.BlockSpec((None, d, ff), wmap),
            pl.BlockSpec((None, d, ff), wmap),
            pl.BlockSpec((None, ff, d), wmap),
            pl.BlockSpec((None, 1, ff), wmap),
            pl.BlockSpec((None, 1, ff), wmap),
            pl.BlockSpec((None, 1, d), wmap),
        ],
        out_specs=pl.BlockSpec((tc * nc, LANES), chunkmap),
        scratch_shapes=[
            pltpu.VMEM((rows * nchp, LANES), jnp.uint32),
            pltpu.VMEM((rows * nchp, LANES), jnp.uint32),
            pltpu.VMEM((rows * nc, LANES), F32),
            pltpu.VMEM((rows * nc, LANES), F32),
        ],
    )
    return pl.pallas_call(
        functools.partial(_moe_kernel, tc=tc),
        grid_spec=grid_spec,
        out_shape=jax.ShapeDtypeStruct((n_tok * nc, LANES), F32),
        compiler_params=_cparams(("arbitrary", "arbitrary")),
        name="moe_ffn",
    )(bexp, n_live, table, h2p, wg, wu, wd, bg, bu, bd)


def _final_kernel(x_ref, moe_ref, mod_ref, g_ref, o_ref):
    tm = x_ref.shape[0]
    x = x_ref[...] + mod_ref[5:6, :] * _unpack_rows(moe_ref, tm)
    o_ref[...] = _rms(x) * g_ref[...]


def _final_norm(x, moe, mod, g_final, *, bsz, seq):
    n_tok, d = x.shape
    tm, _, _, _ = _tiles(seq, n_tok)
    per_b = seq // tm
    nch = d // LANES
    return pl.pallas_call(
        _final_kernel,
        grid=(n_tok // tm,),
        in_specs=[
            pl.BlockSpec((tm, d), lambda i: (i, 0)),
            pl.BlockSpec((tm * nch, LANES), lambda i: (i, 0)),
            pl.BlockSpec((None, N_MOD, d), lambda i: (i // per_b, 0, 0)),
            pl.BlockSpec((1, d), lambda i: (0, 0)),
        ],
        out_specs=pl.BlockSpec((tm, d), lambda i: (i, 0)),
        out_shape=jax.ShapeDtypeStruct((n_tok, d), F32),
        compiler_params=_cparams(("arbitrary",)),
        name="final_norm",
    )(x, moe, mod, g_final)


def _deinterleave_kernel(w_ref, p_ref, g_ref, u_ref):
    for kg in range(w_ref.shape[1] // (2 * LANES)):
        blk = w_ref[:, kg * 2 * LANES:(kg + 1) * 2 * LANES].astype(BF16)
        sel = _dot(blk, p_ref[...])
        g_ref[:, kg * LANES:(kg + 1) * LANES] = sel[:, 0:LANES].astype(BF16)
        u_ref[:, kg * LANES:(kg + 1) * LANES] = sel[:, LANES:2 * LANES].astype(BF16)


def _deinterleave_gate_up(w_gu):
    n_layers, n_exp, d, ff2 = w_gu.shape
    rows = n_layers * n_exp * d
    tr = min(1024, rows)
    sel = np.zeros((2 * LANES, 2 * LANES), np.float32)
    sel[2 * np.arange(LANES), np.arange(LANES)] = 1.0
    sel[2 * np.arange(LANES) + 1, LANES + np.arange(LANES)] = 1.0
    g, u = pl.pallas_call(
        _deinterleave_kernel,
        grid=(rows // tr,),
        in_specs=[pl.BlockSpec((tr, ff2), lambda i: (i, 0)),
                  pl.BlockSpec((2 * LANES, 2 * LANES), lambda i: (0, 0))],
        out_specs=[pl.BlockSpec((tr, ff2 // 2), lambda i: (i, 0)),
                   pl.BlockSpec((tr, ff2 // 2), lambda i: (i, 0))],
        out_shape=[jax.ShapeDtypeStruct((rows, ff2 // 2), BF16),
                   jax.ShapeDtypeStruct((rows, ff2 // 2), BF16)],
        compiler_params=_cparams(("arbitrary",)),
        name="deinterleave_gate_up",
    )(w_gu.reshape(rows, ff2), jnp.asarray(sel, BF16))
    shape = (n_layers, n_exp, d, ff2 // 2)
    return g.reshape(shape), u.reshape(shape)


def _prep_weights(w_in, w_uq, w_ukv, w_a, w_b, w_out, w_router, b_router, w_gu, b_gu, w_dn, b_dn):
    n_layers, d, _ = w_in.shape
    o1 = Q_LORA_RANK
    o2 = o1 + KV_LORA_RANK
    o3 = o2 + QK_ROPE_DIM
    kr = jnp.zeros((n_layers, d, KR_PAD), w_in.dtype).at[:, :, QK_NOPE_DIM:QK_DIM].set(w_in[:, :, o2:o3])
    win = jnp.concatenate([w_in[:, :, :o2], kr, w_in[:, :, o3:]], axis=2).astype(BF16)
    wuq = jnp.pad(w_uq.reshape(n_layers, Q_LORA_RANK, N_HEADS, QK_DIM),
                  ((0, 0), (0, 0), (0, 0), (0, HEAD_PAD - QK_DIM)))
    wuq = wuq.reshape(n_layers, Q_LORA_RANK, N_HEADS * HEAD_PAD).astype(BF16)
    wkv = w_ukv.reshape(n_layers, KV_LORA_RANK, N_HEADS, QK_NOPE_DIM + V_HEAD_DIM)
    wkn = jnp.pad(wkv[..., :QK_NOPE_DIM], ((0, 0), (0, 0), (0, 0), (0, HEAD_PAD - QK_NOPE_DIM)))
    wkn = wkn.reshape(n_layers, KV_LORA_RANK, N_HEADS * HEAD_PAD).astype(BF16)
    wv = wkv[..., QK_NOPE_DIM:].reshape(n_layers, KV_LORA_RANK, N_HEADS * V_HEAD_DIM).astype(BF16)
    n_exp = w_router.shape[2]
    wr = jnp.pad(w_router, ((0, 0), (0, 0), (0, LANES - n_exp)))
    wr_hi = wr.astype(BF16)
    wr_lo = (wr - wr_hi.astype(F32)).astype(BF16)
    br = jnp.pad(b_router, ((0, 0), (0, LANES - n_exp)), constant_values=NEG_BIG)[:, None, :]
    wg, wu = _deinterleave_gate_up(w_gu)
    bg = b_gu[..., 0::2][:, :, None, :]
    bu = b_gu[..., 1::2][:, :, None, :]
    wd = w_dn.astype(BF16)
    bd = b_dn[:, :, None, :]
    return dict(win=win, wuq=wuq, wkn=wkn, wv=wv, wa=w_a.astype(BF16), wb=w_b.astype(BF16),
                wout=w_out.astype(BF16), wr_hi=wr_hi, wr_lo=wr_lo, br=br, wg=wg, wu=wu, bg=bg, bu=bu, wd=wd, bd=bd)


def _trunk(x3, mod, wts, g_mix, g_ffn, g_q, g_kv, g_final, bdc, bds):
    bsz, seq, d = x3.shape
    n_layers = g_mix.shape[0]
    x = x3.reshape(bsz * seq, d)
    tabs = _rope_tabs(seq)
    dft = _dft_consts(seq)
    moe = None
    for l in range(n_layers):
        mod_prev = mod[l - 1] if l > 0 else None
        q, k, vt, w, gates, x = _in_proj(
            x, moe, mod_prev, mod[l], g_mix[l][None, :], wts["win"][l], g_q[l][None, :], wts["wuq"][l],
            g_kv[l][None, :], wts["wkn"][l], wts["wv"][l], tabs, bdc, bds, bsz=bsz, seq=seq)
        ot = _attention(q, k, vt, bsz=bsz, seq=seq)
        yf = _seq_dft_real(w, dft, bsz=bsz, seq=seq)
        x, h2p, top_i, top_g = _post_mix(
            ot, yf, gates, x, mod[l], wts["wa"][l], wts["wb"][l], wts["wout"][l], g_ffn[l][None, :],
            wts["wr_hi"][l], wts["wr_lo"][l], wts["br"][l], bsz=bsz, seq=seq)
        moe = _moe_ffn(h2p, top_i[:, :TOP_K], top_g[:, :TOP_K], wts["wg"][l], wts["wu"][l], wts["wd"][l],
                       wts["bg"][l], wts["bu"][l], wts["bd"][l], d=d)
    y = _final_norm(x, moe, mod[n_layers - 1], g_final[None, :], bsz=bsz, seq=seq)
    return y.reshape(bsz, seq, d)


def kernel(x_prompt, x_sample, c_prompt, c_sample, w_ada, b_ada, g_mix, g_ffn, w_in, g_q, w_uq, g_kv, w_ukv, w_a, w_b, w_out, w_router, b_router, w_gu, b_gu, w_dn, b_dn, g_final):
    n_layers, d, _ = w_ada.shape
    bp, bs = c_prompt.shape[0], c_sample.shape[0]
    rows = -(-(bp + bs) // SUBLANES) * SUBLANES
    c_all = jnp.concatenate([c_prompt, c_sample, jnp.zeros((rows - bp - bs, d), c_prompt.dtype)], axis=0)
    mod = _adaln_mod(c_all, w_ada, b_ada).reshape(n_layers, rows, N_MOD, d)
    wts = _prep_weights(w_in, w_uq, w_ukv, w_a, w_b, w_out, w_router, b_router, w_gu, b_gu, w_dn, b_dn)
    bdc, bds = _channel_dft_consts()
    y_prompt = _trunk(x_prompt, mod[:, :bp], wts, g_mix, g_ffn, g_q, g_kv, g_final, bdc, bds)
    y_sample = _trunk(x_sample, mod[:, bp:bp + bs], wts, g_mix, g_ffn, g_q, g_kv, g_final, bdc, bds)
    return (y_prompt, y_sample)
```

```python
import functools
import math

import numpy as np
import jax
import jax.numpy as jnp
from jax import lax
from jax.experimental import pallas as pl
from jax.experimental.pallas import tpu as pltpu

N_HEADS = 8
QK_NOPE_DIM = 64
QK_ROPE_DIM = 32
V_HEAD_DIM = 64
QK_DIM = QK_NOPE_DIM + QK_ROPE_DIM
Q_LORA_RANK = 384
KV_LORA_RANK = 256
F_GROUPS = 8
F_GROUP_DIM = 64
F_WIDTH = F_GROUPS * F_GROUP_DIM
N_EXPERTS = 32
TOP_K = 4
SWIGLU_LIMIT = 7.0
SWIGLU_ALPHA = 1.702
ROPE_THETA = 10000.0
RMS_EPS = 1e-6
N_MOD = 6

LANES = 128
SUBLANES = 8
VMEM_LIMIT_BYTES = 60 * 1024 * 1024

HEAD_PAD = LANES
KR_PAD = LANES
ROPE_HALF = QK_ROPE_DIM // 2
FFT_N2 = 64
FFT_LANE_TILE = 4096
MOD_COL_TILE = 1536
PREP_ROW_TILE = 1024
MOE_ROWS = 256
SCATTER_BATCH = 4
ATTN_SUM_ROWS = 16
ATTN_HEADS = 4
TOPK_PAD = 8
LOG2_E = math.log2(math.e)
NEG_BIG = -1e30

F32 = jnp.float32
BF16 = jnp.bfloat16


def _tiles(seq, n_tok):
    tm = min(512, seq // 2)
    tk = min(1024, seq // 2)
    tq = min(256, seq)
    tc = min(4096, n_tok)
    return tm, tq, tk, tc


def _cparams(sem, flags=None):
    return pltpu.CompilerParams(dimension_semantics=sem, vmem_limit_bytes=VMEM_LIMIT_BYTES, flags=flags)


def _dot(a, b):
    return jnp.dot(a, b, preferred_element_type=F32)


def _dot_tn(a, b):
    return lax.dot_general(a, b, (((0,), (0,)), ((), ())), preferred_element_type=F32)


def _split_bf16(x):
    hi = x.astype(BF16)
    lo = (x - hi.astype(F32)).astype(BF16)
    return hi, lo


def _rms(x, eps=RMS_EPS):
    return x * lax.rsqrt(jnp.mean(x * x, axis=-1, keepdims=True) + eps)


def _mod_kernel(c_ref, w_ref, b_ref, o_ref):
    c = c_ref[...]
    sc = c * jax.nn.sigmoid(c)
    c_hi, c_lo = _split_bf16(sc)
    w_hi, w_lo = _split_bf16(w_ref[...])
    o_ref[...] = _dot(c_hi, w_hi) + _dot(c_hi, w_lo) + _dot(c_lo, w_hi) + b_ref[...]


def _adaln_mod(c_all, w_ada, b_ada):
    n_layers, d, nd = w_ada.shape
    rows = c_all.shape[0]
    tn = min(nd, MOD_COL_TILE)
    return pl.pallas_call(
        _mod_kernel,
        grid=(n_layers, nd // tn),
        in_specs=[
            pl.BlockSpec((rows, d), lambda l, j: (0, 0)),
            pl.BlockSpec((None, d, tn), lambda l, j: (l, 0, j)),
            pl.BlockSpec((None, 1, tn), lambda l, j: (l, 0, j)),
        ],
        out_specs=pl.BlockSpec((None, rows, tn), lambda l, j: (l, 0, j)),
        out_shape=jax.ShapeDtypeStruct((n_layers, rows, nd), F32),
        compiler_params=_cparams(("arbitrary", "arbitrary")),
        name="adaln_mod",
    )(c_all, w_ada, b_ada.reshape(n_layers, 1, nd))


def _unpack_rows(blk_ref, n_rows):
    nc = blk_ref.shape[0] // n_rows
    blk = blk_ref[...].reshape(n_rows // SUBLANES, nc, SUBLANES, LANES)
    return jnp.concatenate([blk[:, c].reshape(n_rows, LANES) for c in range(nc)], axis=1)


def _pack_rows(x):
    n_rows, width = x.shape
    nc = width // LANES
    parts = [x[:, c * LANES:(c + 1) * LANES].reshape(n_rows // SUBLANES, SUBLANES, LANES) for c in range(nc)]
    return jnp.stack(parts, axis=1).reshape(n_rows * nc, LANES)


def _rope(x, ctab, s1tab, s2tab):
    width = x.shape[1]
    return x * ctab + pltpu.roll(x, width - ROPE_HALF, 1) * s1tab + pltpu.roll(x, ROPE_HALF, 1) * s2tab


def _in_proj_kernel(*refs, d, has_moe):
    if has_moe:
        (x_ref, moe_ref, modp_ref, mod_ref, gmix_ref, win_ref, gq_ref, wuq_ref, gkv_ref, wkn_ref, wv_ref,
         ctab_ref, s1_ref, s2_ref, bdc_ref, bds_ref,
         q_ref, k_ref, vt_ref, w_ref, gates_ref, x2_ref) = refs
    else:
        (x_ref, mod_ref, gmix_ref, win_ref, gq_ref, wuq_ref, gkv_ref, wkn_ref, wv_ref,
         ctab_ref, s1_ref, s2_ref, bdc_ref, bds_ref,
         q_ref, k_ref, vt_ref, w_ref, gates_ref) = refs
    tm = x_ref.shape[0]
    x = x_ref[...]
    if has_moe:
        x = x + modp_ref[5:6, :] * _unpack_rows(moe_ref, tm)
        x2_ref[...] = x
    h = (_rms(x) * gmix_ref[...]) * (1.0 + mod_ref[1:2, :]) + mod_ref[0:1, :]
    hb = h.astype(BF16)

    o1 = Q_LORA_RANK
    o2 = o1 + KV_LORA_RANK
    o3 = o2 + KR_PAD
    o4 = o3 + F_WIDTH
    o5 = o4 + d

    ctab = jnp.concatenate([ctab_ref[...]] * N_HEADS, axis=1)
    s1tab = jnp.concatenate([s1_ref[...]] * N_HEADS, axis=1)
    s2tab = jnp.concatenate([s2_ref[...]] * N_HEADS, axis=1)

    uq = _dot(hb, win_ref[:, 0:o1])
    rq = (_rms(uq) * gq_ref[...]).astype(BF16)
    q = _rope(_dot(rq, wuq_ref[...]), ctab, s1tab, s2tab) * (QK_DIM ** -0.5 * LOG2_E)
    for hd in range(N_HEADS):
        q_ref[hd, :, :] = q[:, hd * HEAD_PAD:(hd + 1) * HEAD_PAD].astype(BF16)

    ukv = _dot(hb, win_ref[:, o1:o2])
    rkv = (_rms(ukv) * gkv_ref[...]).astype(BF16)
    ukr = _dot(hb, win_ref[:, o2:o3])
    kfull = _dot(rkv, wkn_ref[...]) + jnp.concatenate([ukr] * N_HEADS, axis=1)
    kk = _rope(kfull, ctab, s1tab, s2tab)
    for hd in range(N_HEADS):
        k_ref[hd, :, :] = kk[:, hd * HEAD_PAD:(hd + 1) * HEAD_PAD].astype(BF16)

    v = _dot(rkv, wv_ref[...])
    vt_ref[...] = v.T.astype(BF16)

    ufb = _dot(hb, win_ref[:, o3:o4]).astype(BF16)
    w_ref[0, :, :] = _dot(ufb, bdc_ref[...]).astype(BF16)
    w_ref[1, :, :] = (-_dot(ufb, bds_ref[...])).astype(BF16)

    gates_ref[:, 0:d] = jax.nn.sigmoid(_dot(hb, win_ref[:, o4:o5])).astype(BF16)
    gates_ref[:, d:2 * d] = jax.nn.sigmoid(_dot(hb, win_ref[:, o5:o5 + d])).astype(BF16)


def _in_proj(x, moe_prev, mod_prev, mod, gmix, win, gq, wuq, gkv, wkn, wv, tabs, bdc, bds, *, bsz, seq):
    n_tok, d = x.shape
    tm, _, _, _ = _tiles(seq, n_tok)
    has_moe = moe_prev is not None
    per_b = seq // tm
    nch = d // LANES
    win_w = win.shape[1]

    def tok(i):
        return (i, 0)

    def modmap(i):
        return (i // per_b, 0, 0)

    def const2(i):
        return (0, 0)

    def tabmap(i):
        return (i % per_b, 0)

    in_specs = [pl.BlockSpec((tm, d), tok)]
    args = [x]
    if has_moe:
        in_specs += [pl.BlockSpec((tm * nch, LANES), tok),
                     pl.BlockSpec((None, N_MOD, d), modmap)]
        args += [moe_prev, mod_prev]
    in_specs += [
        pl.BlockSpec((None, N_MOD, d), modmap),
        pl.BlockSpec((1, d), const2),
        pl.BlockSpec((d, win_w), const2),
        pl.BlockSpec((1, Q_LORA_RANK), const2),
        pl.BlockSpec((Q_LORA_RANK, N_HEADS * HEAD_PAD), const2),
        pl.BlockSpec((1, KV_LORA_RANK), const2),
        pl.BlockSpec((KV_LORA_RANK, N_HEADS * HEAD_PAD), const2),
        pl.BlockSpec((KV_LORA_RANK, N_HEADS * V_HEAD_DIM), const2),
        pl.BlockSpec((tm, HEAD_PAD), tabmap),
        pl.BlockSpec((tm, HEAD_PAD), tabmap),
        pl.BlockSpec((tm, HEAD_PAD), tabmap),
        pl.BlockSpec((F_WIDTH, F_WIDTH), const2),
        pl.BlockSpec((F_WIDTH, F_WIDTH), const2),
    ]
    args += [mod, gmix, win, gq, wuq, gkv, wkn, wv, tabs[0], tabs[1], tabs[2], bdc, bds]

    hv = N_HEADS * V_HEAD_DIM
    out_specs = [
        pl.BlockSpec((N_HEADS, tm, HEAD_PAD), lambda i: (0, i, 0)),
        pl.BlockSpec((N_HEADS, tm, HEAD_PAD), lambda i: (0, i, 0)),
        pl.BlockSpec((None, hv, tm), lambda i: (i, 0, 0)),
        pl.BlockSpec((2, tm, F_WIDTH), lambda i: (0, i, 0)),
        pl.BlockSpec((tm, 2 * d), tok),
    ]
    out_shape = [
        jax.ShapeDtypeStruct((N_HEADS, n_tok, HEAD_PAD), BF16),
        jax.ShapeDtypeStruct((N_HEADS, n_tok, HEAD_PAD), BF16),
        jax.ShapeDtypeStruct((n_tok // tm, hv, tm), BF16),
        jax.ShapeDtypeStruct((2, n_tok, F_WIDTH), BF16),
        jax.ShapeDtypeStruct((n_tok, 2 * d), BF16),
    ]
    if has_moe:
        out_specs.append(pl.BlockSpec((tm, d), tok))
        out_shape.append(jax.ShapeDtypeStruct((n_tok, d), F32))
    outs = pl.pallas_call(
        functools.partial(_in_proj_kernel, d=d, has_moe=has_moe),
        grid=(n_tok // tm,),
        in_specs=in_specs,
        out_specs=out_specs,
        out_shape=out_shape,
        compiler_params=_cparams(("arbitrary",)),
        name="in_proj",
    )(*args)
    if has_moe:
        return outs
    return (*outs, x)


def _attn_kernel(q_ref, k_ref, vt_ref, o_ref, *bufs, tk, n_kv):
    n_hd, tq, _ = q_ref.shape
    ones = jnp.ones((ATTN_SUM_ROWS, tk), BF16)
    per_tile = vt_ref.shape[2] // tk

    q_t = [q_ref[hd].astype(F32).T.astype(BF16) for hd in range(n_hd)]

    def scores(hd, j):
        return _dot(k_ref[hd, j * tk:(j + 1) * tk, :], q_t[hd]).astype(BF16)

    def update(hd, j, s, m, acc):
        m_new = jnp.maximum(m, jnp.max(s, axis=0, keepdims=True).astype(F32))
        p = jnp.exp2(s - m_new.astype(BF16))
        alpha = jnp.exp2(m - m_new)
        rows = slice(hd * V_HEAD_DIM, (hd + 1) * V_HEAD_DIM)
        if per_tile >= 1:
            vj = vt_ref[j // per_tile, rows, (j % per_tile) * tk:(j % per_tile + 1) * tk]
        else:
            n_t = tk // vt_ref.shape[2]
            vj = jnp.concatenate([vt_ref[j * n_t + t, rows, :] for t in range(n_t)], axis=1)
        v_ext = jnp.concatenate([vj, ones], axis=0)
        return m_new, alpha * acc + _dot(v_ext, p)

    m = [jnp.full((1, tq), NEG_BIG, F32) for _ in range(n_hd)]
    acc = [jnp.zeros((V_HEAD_DIM + ATTN_SUM_ROWS, tq), F32) for _ in range(n_hd)]
    for hd in range(n_hd):
        bufs[2 * hd][...] = scores(hd, 0)
    for j in range(n_kv):
        for hd in range(n_hd):
            if j + 1 < n_kv:
                bufs[2 * hd + (j + 1) % 2][...] = scores(hd, j + 1)
            m[hd], acc[hd] = update(hd, j, bufs[2 * hd + j % 2][...], m[hd], acc[hd])
    for hd in range(n_hd):
        o_ref[hd * V_HEAD_DIM:(hd + 1) * V_HEAD_DIM, :] = (
            acc[hd][0:V_HEAD_DIM] / acc[hd][V_HEAD_DIM:V_HEAD_DIM + 1]).astype(BF16)


def _attention(q, k, vt, *, bsz, seq):
    n_tok = q.shape[1]
    tm, tq, tk, _ = _tiles(seq, n_tok)
    n_q = seq // tq
    n_kv = seq // tk
    n_hd = ATTN_HEADS
    return pl.pallas_call(
        functools.partial(_attn_kernel, tk=tk, n_kv=n_kv),
        grid=(bsz, N_HEADS // n_hd, n_q),
        in_specs=[
            pl.BlockSpec((n_hd, tq, HEAD_PAD), lambda b, h, i: (h, b * n_q + i, 0)),
            pl.BlockSpec((n_hd, seq, HEAD_PAD), lambda b, h, i: (h, b, 0)),
            pl.BlockSpec((seq // tm, n_hd * V_HEAD_DIM, tm), lambda b, h, i: (b, h, 0)),
        ],
        out_specs=pl.BlockSpec((n_hd * V_HEAD_DIM, tq), lambda b, h, i: (h, b * n_q + i)),
        out_shape=jax.ShapeDtypeStruct((N_HEADS * V_HEAD_DIM, n_tok), BF16),
        scratch_shapes=[pltpu.VMEM((tk, tq), BF16) for _ in range(2 * n_hd)],
        compiler_params=_cparams(("arbitrary", "arbitrary", "arbitrary")),
        name="attention",
    )(q, k, vt)


def _fft1_kernel(x_ref, f_ref, tr_ref, ti_ref, o_ref):
    n1 = tr_ref.shape[0]
    a = _dot(f_ref[...], x_ref[...])
    ar, ai = a[:n1], a[n1:]
    tr, ti = tr_ref[...], ti_ref[...]
    o_ref[0, :, :] = (ar * tr - ai * ti).astype(BF16)
    o_ref[1, :, :] = (ar * ti + ai * tr).astype(BF16)


def _fft2_kernel(x_ref, g_ref, o_ref):
    o_ref[...] = _dot(g_ref[...], x_ref[...]).astype(BF16)


def _seq_dft_real(w, consts, *, bsz, seq):
    f1, tr, ti, g2 = consts
    n2 = FFT_N2
    n1 = seq // n2
    fw = w.shape[2]
    lanes = n2 * fw
    lt = min(FFT_LANE_TILE, lanes)
    xw = w.reshape(2, bsz, n1, lanes).transpose(1, 0, 2, 3).reshape(bsz, 2 * n1, lanes)
    b1 = pl.pallas_call(
        _fft1_kernel,
        grid=(bsz, lanes // lt),
        in_specs=[
            pl.BlockSpec((None, 2 * n1, lt), lambda b, j: (b, 0, j)),
            pl.BlockSpec((2 * n1, 2 * n1), lambda b, j: (0, 0)),
            pl.BlockSpec((n1, lt), lambda b, j: (0, j)),
            pl.BlockSpec((n1, lt), lambda b, j: (0, j)),
        ],
        out_specs=pl.BlockSpec((None, 2, n1, lt), lambda b, j: (b, 0, 0, j)),
        out_shape=jax.ShapeDtypeStruct((bsz, 2, n1, lanes), BF16),
        compiler_params=_cparams(("arbitrary", "arbitrary")),
        name="fft_stage1",
    )(xw, f1, tr, ti)
    lanes2 = n1 * fw
    lt2 = min(FFT_LANE_TILE, lanes2)
    bt = b1.reshape(bsz, 2, n1, n2, fw).transpose(0, 1, 3, 2, 4).reshape(bsz, 2 * n2, lanes2)
    y = pl.pallas_call(
        _fft2_kernel,
        grid=(bsz, lanes2 // lt2),
        in_specs=[
            pl.BlockSpec((None, 2 * n2, lt2), lambda b, j: (b, 0, j)),
            pl.BlockSpec((n2, 2 * n2), lambda b, j: (0, 0)),
        ],
        out_specs=pl.BlockSpec((None, n2, lt2), lambda b, j: (b, 0, j)),
        out_shape=jax.ShapeDtypeStruct((bsz, n2, lanes2), BF16),
        compiler_params=_cparams(("arbitrary", "arbitrary")),
        name="fft_stage2",
    )(bt, g2)
    return y.reshape(bsz * seq, fw)


def _dft_consts(seq):
    n2 = FFT_N2
    n1 = seq // n2
    k1 = np.arange(n1)
    ang1 = 2.0 * np.pi * np.outer(k1, k1) / n1
    c1, s1 = np.cos(ang1), np.sin(ang1)
    f1 = np.block([[c1, s1], [-s1, c1]])
    k2 = np.arange(n2)
    ang2 = 2.0 * np.pi * np.outer(k2, k2) / n2
    scale = 1.0 / math.sqrt(seq * F_GROUP_DIM)
    g2 = np.concatenate([np.cos(ang2), np.sin(ang2)], axis=1) * scale
    prod = (jnp.arange(n1, dtype=jnp.int32)[:, None] * jnp.arange(n2, dtype=jnp.int32)[None, :]) % seq
    ang = prod.astype(F32) * (2.0 * math.pi / seq)
    tr = jnp.repeat(jnp.cos(ang), F_WIDTH, axis=1)
    ti = jnp.repeat(-jnp.sin(ang), F_WIDTH, axis=1)
    return jnp.asarray(f1, BF16), tr, ti, jnp.asarray(g2, BF16)


def _channel_dft_consts():
    j = np.arange(F_GROUP_DIM)
    ang = 2.0 * np.pi * np.outer(j, j) / F_GROUP_DIM
    eye = np.eye(F_GROUPS)
    return jnp.asarray(np.kron(eye, np.cos(ang)), BF16), jnp.asarray(np.kron(eye, np.sin(ang)), BF16)


def _rope_tabs(seq):
    inv = 1.0 / (ROPE_THETA ** (jnp.arange(0, QK_ROPE_DIM, 2, dtype=F32) / QK_ROPE_DIM))
    ang = jnp.arange(seq, dtype=F32)[:, None] * inv[None, :]
    cos, sin = jnp.cos(ang), jnp.sin(ang)
    ones = jnp.ones((seq, QK_NOPE_DIM), F32)
    zeros = jnp.zeros((seq, QK_NOPE_DIM), F32)
    pad1 = jnp.ones((seq, HEAD_PAD - QK_DIM), F32)
    pad0 = jnp.zeros((seq, HEAD_PAD - QK_DIM), F32)
    zr = jnp.zeros_like(sin)
    ctab = jnp.concatenate([ones, cos, cos, pad1], axis=1)
    s1tab = jnp.concatenate([zeros, -sin, zr, pad0], axis=1)
    s2tab = jnp.concatenate([zeros, zr, sin, pad0], axis=1)
    return ctab, s1tab, s2tab


def _post_kernel(ot_ref, yf_ref, gates_ref, x_ref, mod_ref, wa_ref, wb_ref, wout_ref, gffn_ref,
                 wrh_ref, wrl_ref, br_ref, x1_ref, h2p_ref, ti_ref, tg_ref, *, d):
    tm = x_ref.shape[0]
    ya = _dot_tn(ot_ref[...], wa_ref[...])
    yb = _dot(yf_ref[...], wb_ref[...])
    merged = gates_ref[:, 0:d].astype(F32) * ya + gates_ref[:, d:2 * d].astype(F32) * yb
    x1 = x_ref[...] + mod_ref[2:3, :] * _dot(merged.astype(BF16), wout_ref[...])
    x1_ref[...] = x1
    h2 = (_rms(x1) * gffn_ref[...]) * (1.0 + mod_ref[4:5, :]) + mod_ref[3:4, :]

    h_hi, h_lo = _split_bf16(h2)
    logits = _dot(h_hi, wrh_ref[...]) + _dot(h_hi, wrl_ref[...]) + _dot(h_lo, wrh_ref[...]) + br_ref[...]
    lane = lax.broadcasted_iota(jnp.int32, logits.shape, 1)
    work = logits
    vals, idxs = [], []
    for _ in range(TOP_K):
        mx = jnp.max(work, axis=1, keepdims=True)
        ix = jnp.min(jnp.where(work == mx, lane, LANES), axis=1, keepdims=True)
        vals.append(mx)
        idxs.append(ix)
        work = jnp.where(lane == ix, -jnp.inf, work)
    exps = [jnp.exp(v - vals[0]) for v in vals]
    tot = exps[0]
    for e in exps[1:]:
        tot = tot + e
    ti = jnp.zeros(logits.shape, jnp.int32)
    tg = jnp.zeros(logits.shape, F32)
    for kk in range(TOP_K):
        ti = jnp.where(lane == kk, idxs[kk], ti)
        tg = jnp.where(lane == kk, exps[kk] / tot, tg)
    ti_ref[...] = ti[:, 0:TOPK_PAD]
    tg_ref[...] = tg[:, 0:TOPK_PAD]

    bits = lax.bitcast_convert_type(h2.astype(BF16).astype(F32), jnp.uint32)
    half = d // 2
    h2p_ref[...] = _pack_rows((bits[:, 0:half] >> 16) | (bits[:, half:d] & jnp.uint32(0xFFFF0000)))


def _post_mix(ot, yf, gates, x, mod, wa, wb, wout, gffn, wrh, wrl, br, *, bsz, seq):
    n_tok, d = x.shape
    tm, _, _, _ = _tiles(seq, n_tok)
    per_b = seq // tm
    hv = N_HEADS * V_HEAD_DIM
    nchp = d // 2 // LANES

    def tok(i):
        return (i, 0)

    def const2(i):
        return (0, 0)

    return pl.pallas_call(
        functools.partial(_post_kernel, d=d),
        grid=(n_tok // tm,),
        in_specs=[
            pl.BlockSpec((hv, tm), lambda i: (0, i)),
            pl.BlockSpec((tm, F_WIDTH), tok),
            pl.BlockSpec((tm, 2 * d), tok),
            pl.BlockSpec((tm, d), tok),
            pl.BlockSpec((None, N_MOD, d), lambda i: (i // per_b, 0, 0)),
            pl.BlockSpec((hv, d), const2),
            pl.BlockSpec((F_WIDTH, d), const2),
            pl.BlockSpec((d, d), const2),
            pl.BlockSpec((1, d), const2),
            pl.BlockSpec((d, LANES), const2),
            pl.BlockSpec((d, LANES), const2),
            pl.BlockSpec((1, LANES), const2),
        ],
        out_specs=[
            pl.BlockSpec((tm, d), tok),
            pl.BlockSpec((tm * nchp, LANES), tok),
            pl.BlockSpec((tm, TOPK_PAD), tok),
            pl.BlockSpec((tm, TOPK_PAD), tok),
        ],
        out_shape=[
            jax.ShapeDtypeStruct((n_tok, d), F32),
            jax.ShapeDtypeStruct((n_tok * nchp, LANES), jnp.uint32),
            jax.ShapeDtypeStruct((n_tok, TOPK_PAD), jnp.int32),
            jax.ShapeDtypeStruct((n_tok, TOPK_PAD), F32),
        ],
        compiler_params=_cparams(("arbitrary",)),
        name="post_mix",
    )(ot, yf, gates, x, mod, wa, wb, wout, gffn, wrh, wrl, br)


def _moe_kernel(bexp_ref, nv_ref, tab_ref, h2p_ref, wgu_ref, wd_ref, bias_ref,
                out_ref, xa_ref, xb_ref, ya_ref, yb_ref, *, tc):
    c = pl.program_id(0)
    j = pl.program_id(1)
    nchp = h2p_ref.shape[0] // tc
    nc = out_ref.shape[0] // tc
    rows = MOE_ROWS
    gp = nchp * SUBLANES
    go = nc * SUBLANES

    @pl.when(j == 0)
    def _():
        out_ref[...] = jnp.zeros_like(out_ref)
        xb_ref[...] = jnp.zeros_like(xb_ref)
        ya_ref[...] = jnp.zeros_like(ya_ref)
        yb_ref[...] = jnp.zeros_like(yb_ref)

    def step(x_nxt, x_cur, y_cur, y_prv):
        for m in range(rows):
            t = tab_ref[0, 0, m]
            x_nxt[pl.ds((m >> 3) * gp + (m & 7), nchp, stride=SUBLANES), :] = (
                h2p_ref[pl.ds((t >> 3) * gp + (t & 7), nchp, stride=SUBLANES), :])
        xs4 = x_cur[...].reshape(rows // SUBLANES, nchp, SUBLANES, LANES)
        los, his = [], []
        for cc in range(nchp):
            wd = xs4[:, cc].reshape(rows, LANES)
            los.append(lax.bitcast_convert_type(wd << 16, F32))
            his.append(lax.bitcast_convert_type(wd & jnp.uint32(0xFFFF0000), F32))
        x = jnp.concatenate(los + his, axis=1).astype(BF16)
        ff = wd_ref.shape[0]
        g = jnp.minimum(_dot(x, wgu_ref[:, 0:ff]) + bias_ref[:, 0:ff], SWIGLU_LIMIT)
        u = jnp.clip(_dot(x, wgu_ref[:, ff:2 * ff]) + bias_ref[:, ff:2 * ff], -SWIGLU_LIMIT, SWIGLU_LIMIT)
        act = ((u + 1.0) * (g * jax.nn.sigmoid(g * SWIGLU_ALPHA))).astype(BF16)
        y_cur[...] = _pack_rows(_dot(act, wd_ref[...]) + bias_ref[:, 2 * ff:])
        for m0 in range(0, rows, SCATTER_BATCH):
            pend = []
            for m in range(m0, m0 + SCATTER_BATCH):
                t = tab_ref[0, 1, m]
                gate = lax.bitcast_convert_type(tab_ref[0, 2, m], F32)
                base = (t >> 3) * go + (t & 7)
                rv = y_prv[pl.ds((m >> 3) * go + (m & 7), nc, stride=SUBLANES), :]
                pend.append((base, out_ref[pl.ds(base, nc, stride=SUBLANES), :] + rv * gate))
            for base, val in pend:
                out_ref[pl.ds(base, nc, stride=SUBLANES), :] = val

    live = j < nv_ref[c] + 2

    @pl.when(jnp.logical_and(live, j % 2 == 0))
    def _():
        step(xa_ref, xb_ref, yb_ref, ya_ref)

    @pl.when(jnp.logical_and(live, j % 2 == 1))
    def _():
        step(xb_ref, xa_ref, ya_ref, yb_ref)


def _moe_ffn(h2p, top_i, top_g, wgu, wd, bias, *, d):
    n_tok = top_i.shape[0]
    nchp = h2p.shape[0] // n_tok
    _, _, _, tc = _tiles(n_tok, n_tok)
    nc = d // LANES
    rows = MOE_ROWS
    n_chunks = n_tok // tc
    n_asg = tc * TOP_K
    n_blk = n_asg // rows + N_EXPERTS
    n_exp, ff, _ = wd.shape

    e_flat = top_i.reshape(n_chunks, n_asg)
    g_flat = top_g.reshape(n_chunks, n_asg)
    key = e_flat * n_asg + jnp.arange(n_asg, dtype=jnp.int32)[None, :]
    key = jnp.sort(key, axis=1)
    order = key % n_asg
    counts = jnp.sum((e_flat[:, :, None] == jnp.arange(n_exp, dtype=jnp.int32)[None, None, :]).astype(jnp.int32),
                     axis=1)
    start = jnp.cumsum(counts, axis=1) - counts
    nb_e = (counts + rows - 1) // rows
    blk_end = jnp.cumsum(nb_e, axis=1)
    blk_start = blk_end - nb_e
    jb = jnp.arange(n_blk, dtype=jnp.int32)
    e_of_blk = jnp.sum((jb[None, :, None] >= blk_end[:, None, :]).astype(jnp.int32), axis=2)
    blk_valid = e_of_blk < n_exp
    last_e = jnp.max(jnp.where(counts > 0, jnp.arange(n_exp, dtype=jnp.int32)[None, :], 0), axis=1, keepdims=True)
    e_blk = jnp.where(blk_valid, jnp.minimum(e_of_blk, n_exp - 1), last_e)
    q0 = (jb[None, :] - jnp.take_along_axis(blk_start, e_blk, axis=1)) * rows
    cnt_blk = jnp.take_along_axis(counts, e_blk, axis=1)
    n_valid = jnp.where(blk_valid, jnp.clip(cnt_blk - q0, 0, rows), 0)
    r = jnp.arange(rows, dtype=jnp.int32)
    q = q0[:, :, None] + r[None, None, :]
    row_ok = r[None, None, :] < n_valid[:, :, None]
    s_idx = jnp.clip(jnp.take_along_axis(start, e_blk, axis=1)[:, :, None] + q, 0, n_asg - 1)
    a_idx = jnp.take_along_axis(order, s_idx.reshape(n_chunks, n_blk * rows), axis=1)
    row_tok = jnp.where(row_ok.reshape(n_chunks, -1), a_idx // TOP_K, 0).astype(jnp.int32)
    row_gate = jnp.where(row_ok.reshape(n_chunks, -1), jnp.take_along_axis(g_flat, a_idx, axis=1), 0.0)
    n_steps = n_blk + 2
    row_tok = row_tok.reshape(n_chunks, n_blk, rows)
    row_gate = row_gate.reshape(n_chunks, n_blk, rows).astype(F32)
    pad_i = jnp.zeros((n_chunks, 2, rows), jnp.int32)
    gate_bits = lax.bitcast_convert_type(row_gate, jnp.int32)
    table = jnp.stack([jnp.concatenate([row_tok, pad_i], axis=1),
                       jnp.concatenate([pad_i, row_tok], axis=1),
                       jnp.concatenate([pad_i, gate_bits], axis=1)], axis=2)
    table = table.reshape(n_chunks * n_steps, 3, rows)
    e_blk = e_blk.astype(jnp.int32)
    bexp = jnp.concatenate([e_blk[:, :1], e_blk, e_blk[:, -1:]], axis=1).reshape(-1)
    n_live = blk_end[:, -1].astype(jnp.int32)

    def wmap(c, j, bexp_ref, nv_ref):
        return (bexp_ref[c * n_steps + j], 0, 0)

    def rowmap(c, j, bexp_ref, nv_ref):
        return (c * n_steps + j, 0, 0)

    def chunkmap(c, j, bexp_ref, nv_ref):
        return (c, 0)

    grid_spec = pltpu.PrefetchScalarGridSpec(
        num_scalar_prefetch=2,
        grid=(n_chunks, n_steps),
        in_specs=[
            pl.BlockSpec((1, 3, rows), rowmap, memory_space=pltpu.SMEM),
            pl.BlockSpec((tc * nchp, LANES), chunkmap, pipeline_mode=pl.Buffered(1)),
            pl.BlockSpec((None, d, 2 * ff), wmap),
            pl.BlockSpec((None, ff, d), wmap),
            pl.BlockSpec((None, 1, 2 * ff + d), wmap),
        ],
        out_specs=pl.BlockSpec((tc * nc, LANES), chunkmap),
        scratch_shapes=[
            pltpu.VMEM((rows * nchp, LANES), jnp.uint32),
            pltpu.VMEM((rows * nchp, LANES), jnp.uint32),
            pltpu.VMEM((rows * nc, LANES), F32),
            pltpu.VMEM((rows * nc, LANES), F32),
        ],
    )
    return pl.pallas_call(
        functools.partial(_moe_kernel, tc=tc),
        grid_spec=grid_spec,
        out_shape=jax.ShapeDtypeStruct((n_tok * nc, LANES), F32),
        compiler_params=_cparams(("arbitrary", "arbitrary")),
        name="moe_ffn",
    )(bexp, n_live, table, h2p, wgu, wd, bias)


def _final_kernel(x_ref, moe_ref, mod_ref, g_ref, o_ref):
    tm = x_ref.shape[0]
    x = x_ref[...] + mod_ref[5:6, :] * _unpack_rows(moe_ref, tm)
    o_ref[...] = _rms(x) * g_ref[...]


def _final_norm(x, moe, mod, g_final, *, bsz, seq):
    n_tok, d = x.shape
    tm, _, _, _ = _tiles(seq, n_tok)
    per_b = seq // tm
    nch = d // LANES
    return pl.pallas_call(
        _final_kernel,
        grid=(n_tok // tm,),
        in_specs=[
            pl.BlockSpec((tm, d), lambda i: (i, 0)),
            pl.BlockSpec((tm * nch, LANES), lambda i: (i, 0)),
            pl.BlockSpec((None, N_MOD, d), lambda i: (i // per_b, 0, 0)),
            pl.BlockSpec((1, d), lambda i: (0, 0)),
        ],
        out_specs=pl.BlockSpec((tm, d), lambda i: (i, 0)),
        out_shape=jax.ShapeDtypeStruct((n_tok, d), F32),
        compiler_params=_cparams(("arbitrary",)),
        name="final_norm",
    )(x, moe, mod, g_final)


def _deinterleave_kernel(w_ref, p_ref, o_ref):
    ff = w_ref.shape[1] // 2
    for kg in range(ff // LANES):
        blk = w_ref[:, kg * 2 * LANES:(kg + 1) * 2 * LANES].astype(BF16)
        sel = _dot(blk, p_ref[...])
        o_ref[:, kg * LANES:(kg + 1) * LANES] = sel[:, 0:LANES].astype(BF16)
        o_ref[:, ff + kg * LANES:ff + (kg + 1) * LANES] = sel[:, LANES:2 * LANES].astype(BF16)


def _deinterleave_gate_up(w_gu):
    n_layers, n_exp, d, ff2 = w_gu.shape
    rows = n_layers * n_exp * d
    tr = min(PREP_ROW_TILE, rows)
    sel = np.zeros((2 * LANES, 2 * LANES), np.float32)
    sel[2 * np.arange(LANES), np.arange(LANES)] = 1.0
    sel[2 * np.arange(LANES) + 1, LANES + np.arange(LANES)] = 1.0
    out = pl.pallas_call(
        _deinterleave_kernel,
        grid=(rows // tr,),
        in_specs=[pl.BlockSpec((tr, ff2), lambda i: (i, 0)),
                  pl.BlockSpec((2 * LANES, 2 * LANES), lambda i: (0, 0))],
        out_specs=pl.BlockSpec((tr, ff2), lambda i: (i, 0)),
        out_shape=jax.ShapeDtypeStruct((rows, ff2), BF16),
        compiler_params=_cparams(("arbitrary",)),
        name="deinterleave_gate_up",
    )(w_gu.reshape(rows, ff2), jnp.asarray(sel, BF16))
    return out.reshape(n_layers, n_exp, d, ff2)


def _prep_weights(w_in, w_uq, w_ukv, w_a, w_b, w_out, w_router, b_router, w_gu, b_gu, w_dn, b_dn):
    n_layers, d, _ = w_in.shape
    o1 = Q_LORA_RANK
    o2 = o1 + KV_LORA_RANK
    o3 = o2 + QK_ROPE_DIM
    kr = jnp.zeros((n_layers, d, KR_PAD), w_in.dtype).at[:, :, QK_NOPE_DIM:QK_DIM].set(w_in[:, :, o2:o3])
    win = jnp.concatenate([w_in[:, :, :o2], kr, w_in[:, :, o3:]], axis=2).astype(BF16)
    wuq = jnp.pad(w_uq.reshape(n_layers, Q_LORA_RANK, N_HEADS, QK_DIM),
                  ((0, 0), (0, 0), (0, 0), (0, HEAD_PAD - QK_DIM)))
    wuq = wuq.reshape(n_layers, Q_LORA_RANK, N_HEADS * HEAD_PAD).astype(BF16)
    wkv = w_ukv.reshape(n_layers, KV_LORA_RANK, N_HEADS, QK_NOPE_DIM + V_HEAD_DIM)
    wkn = jnp.pad(wkv[..., :QK_NOPE_DIM], ((0, 0), (0, 0), (0, 0), (0, HEAD_PAD - QK_NOPE_DIM)))
    wkn = wkn.reshape(n_layers, KV_LORA_RANK, N_HEADS * HEAD_PAD).astype(BF16)
    wv = wkv[..., QK_NOPE_DIM:].reshape(n_layers, KV_LORA_RANK, N_HEADS * V_HEAD_DIM).astype(BF16)
    n_exp = w_router.shape[2]
    wr = jnp.pad(w_router, ((0, 0), (0, 0), (0, LANES - n_exp)))
    wr_hi = wr.astype(BF16)
    wr_lo = (wr - wr_hi.astype(F32)).astype(BF16)
    br = jnp.pad(b_router, ((0, 0), (0, LANES - n_exp)), constant_values=NEG_BIG)[:, None, :]
    wgu = _deinterleave_gate_up(w_gu)
    bias = jnp.concatenate([b_gu[..., 0::2], b_gu[..., 1::2], b_dn], axis=-1)[:, :, None, :]
    wd = w_dn.astype(BF16)
    return dict(win=win, wuq=wuq, wkn=wkn, wv=wv, wa=w_a.astype(BF16), wb=w_b.astype(BF16),
                wout=w_out.astype(BF16), wr_hi=wr_hi, wr_lo=wr_lo, br=br, wgu=wgu, wd=wd, bias=bias)


def _trunk(x3, mod, wts, g_mix, g_ffn, g_q, g_kv, g_final, bdc, bds):
    bsz, seq, d = x3.shape
    n_layers = g_mix.shape[0]
    x = x3.reshape(bsz * seq, d)
    tabs = _rope_tabs(seq)
    dft = _dft_consts(seq)
    moe = None
    for l in range(n_layers):
        mod_prev = mod[l - 1] if l > 0 else None
        q, k, vt, w, gates, x = _in_proj(
            x, moe, mod_prev, mod[l], g_mix[l][None, :], wts["win"][l], g_q[l][None, :], wts["wuq"][l],
            g_kv[l][None, :], wts["wkn"][l], wts["wv"][l], tabs, bdc, bds, bsz=bsz, seq=seq)
        ot = _attention(q, k, vt, bsz=bsz, seq=seq)
        yf = _seq_dft_real(w, dft, bsz=bsz, seq=seq)
        x, h2p, top_i, top_g = _post_mix(
            ot, yf, gates, x, mod[l], wts["wa"][l], wts["wb"][l], wts["wout"][l], g_ffn[l][None, :],
            wts["wr_hi"][l], wts["wr_lo"][l], wts["br"][l], bsz=bsz, seq=seq)
        moe = _moe_ffn(h2p, top_i[:, :TOP_K], top_g[:, :TOP_K], wts["wgu"][l], wts["wd"][l], wts["bias"][l], d=d)
    y = _final_norm(x, moe, mod[n_layers - 1], g_final[None, :], bsz=bsz, seq=seq)
    return y.reshape(bsz, seq, d)


def kernel(x_prompt, x_sample, c_prompt, c_sample, w_ada, b_ada, g_mix, g_ffn, w_in, g_q, w_uq, g_kv, w_ukv, w_a, w_b, w_out, w_router, b_router, w_gu, b_gu, w_dn, b_dn, g_final):
    n_layers, d, _ = w_ada.shape
    bp, bs = c_prompt.shape[0], c_sample.shape[0]
    rows = -(-(bp + bs) // SUBLANES) * SUBLANES
    c_all = jnp.concatenate([c_prompt, c_sample, jnp.zeros((rows - bp - bs, d), c_prompt.dtype)], axis=0)
    mod = _adaln_mod(c_all, w_ada, b_ada).reshape(n_layers, rows, N_MOD, d)
    wts = _prep_weights(w_in, w_uq, w_ukv, w_a, w_b, w_out, w_router, b_router, w_gu, b_gu, w_dn, b_dn)
    bdc, bds = _channel_dft_consts()
    y_prompt = _trunk(x_prompt, mod[:, :bp], wts, g_mix, g_ffn, g_q, g_kv, g_final, bdc, bds)
    y_sample = _trunk(x_sample, mod[:, bp:bp + bs], wts, g_mix, g_ffn, g_q, g_kv, g_final, bdc, bds)
    return (y_prompt, y_sample)
```

```python
import functools
import math

import numpy as np
import jax
import jax.numpy as jnp
from jax import lax
from jax.experimental import pallas as pl
from jax.experimental.pallas import tpu as pltpu

N_HEADS = 8
QK_NOPE_DIM = 64
QK_ROPE_DIM = 32
V_HEAD_DIM = 64
QK_DIM = QK_NOPE_DIM + QK_ROPE_DIM
Q_LORA_RANK = 384
KV_LORA_RANK = 256
F_GROUPS = 8
F_GROUP_DIM = 64
F_WIDTH = F_GROUPS * F_GROUP_DIM
N_EXPERTS = 32
TOP_K = 4
SWIGLU_LIMIT = 7.0
SWIGLU_ALPHA = 1.702
ROPE_THETA = 10000.0
RMS_EPS = 1e-6
N_MOD = 6

LANES = 128
SUBLANES = 8
VMEM_LIMIT_BYTES = 60 * 1024 * 1024

HEAD_PAD = LANES
KR_PAD = LANES
ROPE_HALF = QK_ROPE_DIM // 2
FFT_N2 = 64
FFT_LANE_TILE = 4096
MOD_COL_TILE = 1536
PREP_ROW_TILE = 1024
MOE_ROWS = 256
SCATTER_BATCH = 4
ATTN_SUM_ROWS = 16
ATTN_HEADS = 4
TOPK_PAD = 8
LOG2_E = math.log2(math.e)
NEG_BIG = -1e30

F32 = jnp.float32
BF16 = jnp.bfloat16


def _tiles(seq, n_tok):
    tm = min(512, seq // 2)
    tk = min(512, seq // 2)
    tq = min(256, seq)
    tc = min(4096, n_tok)
    return tm, tq, tk, tc


def _cparams(sem, flags=None):
    return pltpu.CompilerParams(dimension_semantics=sem, vmem_limit_bytes=VMEM_LIMIT_BYTES, flags=flags)


def _dot(a, b):
    return jnp.dot(a, b, preferred_element_type=F32)


def _dot_tn(a, b):
    return lax.dot_general(a, b, (((0,), (0,)), ((), ())), preferred_element_type=F32)


def _split_bf16(x):
    hi = x.astype(BF16)
    lo = (x - hi.astype(F32)).astype(BF16)
    return hi, lo


def _rms(x, eps=RMS_EPS):
    return x * lax.rsqrt(jnp.mean(x * x, axis=-1, keepdims=True) + eps)


def _mod_kernel(c_ref, w_ref, b_ref, o_ref):
    c = c_ref[...]
    sc = c * jax.nn.sigmoid(c)
    c_hi, c_lo = _split_bf16(sc)
    w_hi, w_lo = _split_bf16(w_ref[...])
    o_ref[...] = _dot(c_hi, w_hi) + _dot(c_hi, w_lo) + _dot(c_lo, w_hi) + b_ref[...]


def _adaln_mod(c_all, w_ada, b_ada):
    n_layers, d, nd = w_ada.shape
    rows = c_all.shape[0]
    tn = min(nd, MOD_COL_TILE)
    return pl.pallas_call(
        _mod_kernel,
        grid=(n_layers, nd // tn),
        in_specs=[
            pl.BlockSpec((rows, d), lambda l, j: (0, 0)),
            pl.BlockSpec((None, d, tn), lambda l, j: (l, 0, j)),
            pl.BlockSpec((None, 1, tn), lambda l, j: (l, 0, j)),
        ],
        out_specs=pl.BlockSpec((None, rows, tn), lambda l, j: (l, 0, j)),
        out_shape=jax.ShapeDtypeStruct((n_layers, rows, nd), F32),
        compiler_params=_cparams(("arbitrary", "arbitrary")),
        name="adaln_mod",
    )(c_all, w_ada, b_ada.reshape(n_layers, 1, nd))


def _unpack_rows(blk_ref, n_rows):
    nc = blk_ref.shape[0] // n_rows
    blk = blk_ref[...].reshape(n_rows // SUBLANES, nc, SUBLANES, LANES)
    return jnp.concatenate([blk[:, c].reshape(n_rows, LANES) for c in range(nc)], axis=1)


def _pack_rows(x):
    n_rows, width = x.shape
    nc = width // LANES
    parts = [x[:, c * LANES:(c + 1) * LANES].reshape(n_rows // SUBLANES, SUBLANES, LANES) for c in range(nc)]
    return jnp.stack(parts, axis=1).reshape(n_rows * nc, LANES)


def _rope(x, ctab, s1tab, s2tab):
    width = x.shape[1]
    return x * ctab + pltpu.roll(x, width - ROPE_HALF, 1) * s1tab + pltpu.roll(x, ROPE_HALF, 1) * s2tab


def _in_proj_kernel(*refs, d, has_moe):
    if has_moe:
        (x_ref, moe_ref, modp_ref, mod_ref, gmix_ref, win_ref, gq_ref, wuq_ref, gkv_ref, wkn_ref, wv_ref,
         ctab_ref, s1_ref, s2_ref, bdc_ref, bds_ref,
         q_ref, k_ref, vt_ref, w_ref, gates_ref, x2_ref) = refs
    else:
        (x_ref, mod_ref, gmix_ref, win_ref, gq_ref, wuq_ref, gkv_ref, wkn_ref, wv_ref,
         ctab_ref, s1_ref, s2_ref, bdc_ref, bds_ref,
         q_ref, k_ref, vt_ref, w_ref, gates_ref) = refs
    tm = x_ref.shape[0]
    x = x_ref[...]
    if has_moe:
        x = x + modp_ref[5:6, :] * _unpack_rows(moe_ref, tm)
        x2_ref[...] = x
    h = (_rms(x) * gmix_ref[...]) * (1.0 + mod_ref[1:2, :]) + mod_ref[0:1, :]
    hb = h.astype(BF16)

    o1 = Q_LORA_RANK
    o2 = o1 + KV_LORA_RANK
    o3 = o2 + KR_PAD
    o4 = o3 + F_WIDTH
    o5 = o4 + d

    ctab = jnp.concatenate([ctab_ref[...]] * N_HEADS, axis=1)
    s1tab = jnp.concatenate([s1_ref[...]] * N_HEADS, axis=1)
    s2tab = jnp.concatenate([s2_ref[...]] * N_HEADS, axis=1)

    uq = _dot(hb, win_ref[:, 0:o1])
    rq = (_rms(uq) * gq_ref[...]).astype(BF16)
    q = _rope(_dot(rq, wuq_ref[...]), ctab, s1tab, s2tab) * (QK_DIM ** -0.5 * LOG2_E)
    for hd in range(N_HEADS):
        q_ref[hd, :, :] = q[:, hd * HEAD_PAD:(hd + 1) * HEAD_PAD].astype(BF16)

    ukv = _dot(hb, win_ref[:, o1:o2])
    rkv = (_rms(ukv) * gkv_ref[...]).astype(BF16)
    ukr = _dot(hb, win_ref[:, o2:o3])
    kfull = _dot(rkv, wkn_ref[...]) + jnp.concatenate([ukr] * N_HEADS, axis=1)
    kk = _rope(kfull, ctab, s1tab, s2tab)
    for hd in range(N_HEADS):
        k_ref[hd, :, :] = kk[:, hd * HEAD_PAD:(hd + 1) * HEAD_PAD].astype(BF16)

    v = _dot(rkv, wv_ref[...])
    vt_ref[...] = v.T.astype(BF16)

    ufb = _dot(hb, win_ref[:, o3:o4]).astype(BF16)
    w_ref[0, :, :] = _dot(ufb, bdc_ref[...]).astype(BF16)
    w_ref[1, :, :] = (-_dot(ufb, bds_ref[...])).astype(BF16)

    gates_ref[:, 0:d] = jax.nn.sigmoid(_dot(hb, win_ref[:, o4:o5])).astype(BF16)
    gates_ref[:, d:2 * d] = jax.nn.sigmoid(_dot(hb, win_ref[:, o5:o5 + d])).astype(BF16)


def _in_proj(x, moe_prev, mod_prev, mod, gmix, win, gq, wuq, gkv, wkn, wv, tabs, bdc, bds, *, bsz, seq):
    n_tok, d = x.shape
    tm, _, _, _ = _tiles(seq, n_tok)
    has_moe = moe_prev is not None
    per_b = seq // tm
    nch = d // LANES
    win_w = win.shape[1]

    def tok(i):
        return (i, 0)

    def modmap(i):
        return (i // per_b, 0, 0)

    def const2(i):
        return (0, 0)

    def tabmap(i):
        return (i % per_b, 0)

    in_specs = [pl.BlockSpec((tm, d), tok)]
    args = [x]
    if has_moe:
        in_specs += [pl.BlockSpec((tm * nch, LANES), tok),
                     pl.BlockSpec((None, N_MOD, d), modmap)]
        args += [moe_prev, mod_prev]
    in_specs += [
        pl.BlockSpec((None, N_MOD, d), modmap),
        pl.BlockSpec((1, d), const2),
        pl.BlockSpec((d, win_w), const2),
        pl.BlockSpec((1, Q_LORA_RANK), const2),
        pl.BlockSpec((Q_LORA_RANK, N_HEADS * HEAD_PAD), const2),
        pl.BlockSpec((1, KV_LORA_RANK), const2),
        pl.BlockSpec((KV_LORA_RANK, N_HEADS * HEAD_PAD), const2),
        pl.BlockSpec((KV_LORA_RANK, N_HEADS * V_HEAD_DIM), const2),
        pl.BlockSpec((tm, HEAD_PAD), tabmap),
        pl.BlockSpec((tm, HEAD_PAD), tabmap),
        pl.BlockSpec((tm, HEAD_PAD), tabmap),
        pl.BlockSpec((F_WIDTH, F_WIDTH), const2),
        pl.BlockSpec((F_WIDTH, F_WIDTH), const2),
    ]
    args += [mod, gmix, win, gq, wuq, gkv, wkn, wv, tabs[0], tabs[1], tabs[2], bdc, bds]

    hv = N_HEADS * V_HEAD_DIM
    out_specs = [
        pl.BlockSpec((N_HEADS, tm, HEAD_PAD), lambda i: (0, i, 0)),
        pl.BlockSpec((N_HEADS, tm, HEAD_PAD), lambda i: (0, i, 0)),
        pl.BlockSpec((None, hv, tm), lambda i: (i, 0, 0)),
        pl.BlockSpec((2, tm, F_WIDTH), lambda i: (0, i, 0)),
        pl.BlockSpec((tm, 2 * d), tok),
    ]
    out_shape = [
        jax.ShapeDtypeStruct((N_HEADS, n_tok, HEAD_PAD), BF16),
        jax.ShapeDtypeStruct((N_HEADS, n_tok, HEAD_PAD), BF16),
        jax.ShapeDtypeStruct((n_tok // tm, hv, tm), BF16),
        jax.ShapeDtypeStruct((2, n_tok, F_WIDTH), BF16),
        jax.ShapeDtypeStruct((n_tok, 2 * d), BF16),
    ]
    if has_moe:
        out_specs.append(pl.BlockSpec((tm, d), tok))
        out_shape.append(jax.ShapeDtypeStruct((n_tok, d), F32))
    outs = pl.pallas_call(
        functools.partial(_in_proj_kernel, d=d, has_moe=has_moe),
        grid=(n_tok // tm,),
        in_specs=in_specs,
        out_specs=out_specs,
        out_shape=out_shape,
        compiler_params=_cparams(("arbitrary",)),
        name="in_proj",
    )(*args)
    if has_moe:
        return outs
    return (*outs, x)


def _attn_kernel(q_ref, k_ref, vt_ref, o_ref, *bufs, tk, n_kv):
    n_hd, tq, _ = q_ref.shape
    ones = jnp.ones((ATTN_SUM_ROWS, tk), BF16)
    per_tile = vt_ref.shape[2] // tk

    q_t = [q_ref[hd].astype(F32).T.astype(BF16) for hd in range(n_hd)]

    def scores(hd, j):
        return _dot(k_ref[hd, j * tk:(j + 1) * tk, :], q_t[hd]).astype(BF16)

    def update(hd, j, s, m, acc):
        m_new = jnp.maximum(m, jnp.max(s, axis=0, keepdims=True).astype(F32))
        p = jnp.exp2(s - m_new.astype(BF16))
        alpha = jnp.exp2(m - m_new)
        rows = slice(hd * V_HEAD_DIM, (hd + 1) * V_HEAD_DIM)
        if per_tile >= 1:
            vj = vt_ref[j // per_tile, rows, (j % per_tile) * tk:(j % per_tile + 1) * tk]
        else:
            n_t = tk // vt_ref.shape[2]
            vj = jnp.concatenate([vt_ref[j * n_t + t, rows, :] for t in range(n_t)], axis=1)
        v_ext = jnp.concatenate([vj, ones], axis=0)
        return m_new, alpha * acc + _dot(v_ext, p)

    m = [jnp.full((1, tq), NEG_BIG, F32) for _ in range(n_hd)]
    acc = [jnp.zeros((V_HEAD_DIM + ATTN_SUM_ROWS, tq), F32) for _ in range(n_hd)]
    for hd in range(n_hd):
        bufs[2 * hd][...] = scores(hd, 0)
    for j in range(n_kv):
        for hd in range(n_hd):
            if j + 1 < n_kv:
                bufs[2 * hd + (j + 1) % 2][...] = scores(hd, j + 1)
            m[hd], acc[hd] = update(hd, j, bufs[2 * hd + j % 2][...], m[hd], acc[hd])
    for hd in range(n_hd):
        o_ref[hd * V_HEAD_DIM:(hd + 1) * V_HEAD_DIM, :] = (
            acc[hd][0:V_HEAD_DIM] / acc[hd][V_HEAD_DIM:V_HEAD_DIM + 1]).astype(BF16)


def _attention(q, k, vt, *, bsz, seq):
    n_tok = q.shape[1]
    tm, tq, tk, _ = _tiles(seq, n_tok)
    n_q = seq // tq
    n_kv = seq // tk
    n_hd = ATTN_HEADS
    return pl.pallas_call(
        functools.partial(_attn_kernel, tk=tk, n_kv=n_kv),
        grid=(bsz, N_HEADS // n_hd, n_q),
        in_specs=[
            pl.BlockSpec((n_hd, tq, HEAD_PAD), lambda b, h, i: (h, b * n_q + i, 0)),
            pl.BlockSpec((n_hd, seq, HEAD_PAD), lambda b, h, i: (h, b, 0)),
            pl.BlockSpec((seq // tm, n_hd * V_HEAD_DIM, tm), lambda b, h, i: (b, h, 0)),
        ],
        out_specs=pl.BlockSpec((n_hd * V_HEAD_DIM, tq), lambda b, h, i: (h, b * n_q + i)),
        out_shape=jax.ShapeDtypeStruct((N_HEADS * V_HEAD_DIM, n_tok), BF16),
        scratch_shapes=[pltpu.VMEM((tk, tq), BF16) for _ in range(2 * n_hd)],
        compiler_params=_cparams(("arbitrary", "arbitrary", "arbitrary")),
        name="attention",
    )(q, k, vt)


def _fft1_kernel(x_ref, f_ref, tr_ref, ti_ref, o_ref):
    n1 = tr_ref.shape[0]
    a = _dot(f_ref[...], x_ref[...])
    ar, ai = a[:n1], a[n1:]
    tr, ti = tr_ref[...], ti_ref[...]
    o_ref[0, :, :] = (ar * tr - ai * ti).astype(BF16)
    o_ref[1, :, :] = (ar * ti + ai * tr).astype(BF16)


def _fft2_kernel(x_ref, g_ref, o_ref):
    o_ref[...] = _dot(g_ref[...], x_ref[...]).astype(BF16)


def _seq_dft_real(w, consts, *, bsz, seq):
    f1, tr, ti, g2 = consts
    n2 = FFT_N2
    n1 = seq // n2
    fw = w.shape[2]
    lanes = n2 * fw
    lt = min(FFT_LANE_TILE, lanes)
    xw = w.reshape(2, bsz, n1, lanes).transpose(1, 0, 2, 3).reshape(bsz, 2 * n1, lanes)
    b1 = pl.pallas_call(
        _fft1_kernel,
        grid=(bsz, lanes // lt),
        in_specs=[
            pl.BlockSpec((None, 2 * n1, lt), lambda b, j: (b, 0, j)),
            pl.BlockSpec((2 * n1, 2 * n1), lambda b, j: (0, 0)),
            pl.BlockSpec((n1, lt), lambda b, j: (0, j)),
            pl.BlockSpec((n1, lt), lambda b, j: (0, j)),
        ],
        out_specs=pl.BlockSpec((None, 2, n1, lt), lambda b, j: (b, 0, 0, j)),
        out_shape=jax.ShapeDtypeStruct((bsz, 2, n1, lanes), BF16),
        compiler_params=_cparams(("arbitrary", "arbitrary")),
        name="fft_stage1",
    )(xw, f1, tr, ti)
    lanes2 = n1 * fw
    lt2 = min(FFT_LANE_TILE, lanes2)
    bt = b1.reshape(bsz, 2, n1, n2, fw).transpose(0, 1, 3, 2, 4).reshape(bsz, 2 * n2, lanes2)
    y = pl.pallas_call(
        _fft2_kernel,
        grid=(bsz, lanes2 // lt2),
        in_specs=[
            pl.BlockSpec((None, 2 * n2, lt2), lambda b, j: (b, 0, j)),
            pl.BlockSpec((n2, 2 * n2), lambda b, j: (0, 0)),
        ],
        out_specs=pl.BlockSpec((None, n2, lt2), lambda b, j: (b, 0, j)),
        out_shape=jax.ShapeDtypeStruct((bsz, n2, lanes2), BF16),
        compiler_params=_cparams(("arbitrary", "arbitrary")),
        name="fft_stage2",
    )(bt, g2)
    return y.reshape(bsz * seq, fw)


def _dft_consts(seq):
    n2 = FFT_N2
    n1 = seq // n2
    k1 = np.arange(n1)
    ang1 = 2.0 * np.pi * np.outer(k1, k1) / n1
    c1, s1 = np.cos(ang1), np.sin(ang1)
    f1 = np.block([[c1, s1], [-s1, c1]])
    k2 = np.arange(n2)
    ang2 = 2.0 * np.pi * np.outer(k2, k2) / n2
    scale = 1.0 / math.sqrt(seq * F_GROUP_DIM)
    g2 = np.concatenate([np.cos(ang2), np.sin(ang2)], axis=1) * scale
    prod = (jnp.arange(n1, dtype=jnp.int32)[:, None] * jnp.arange(n2, dtype=jnp.int32)[None, :]) % seq
    ang = prod.astype(F32) * (2.0 * math.pi / seq)
    tr = jnp.repeat(jnp.cos(ang), F_WIDTH, axis=1)
    ti = jnp.repeat(-jnp.sin(ang), F_WIDTH, axis=1)
    return jnp.asarray(f1, BF16), tr, ti, jnp.asarray(g2, BF16)


def _channel_dft_consts():
    j = np.arange(F_GROUP_DIM)
    ang = 2.0 * np.pi * np.outer(j, j) / F_GROUP_DIM
    eye = np.eye(F_GROUPS)
    return jnp.asarray(np.kron(eye, np.cos(ang)), BF16), jnp.asarray(np.kron(eye, np.sin(ang)), BF16)


def _rope_tabs(seq):
    inv = 1.0 / (ROPE_THETA ** (jnp.arange(0, QK_ROPE_DIM, 2, dtype=F32) / QK_ROPE_DIM))
    ang = jnp.arange(seq, dtype=F32)[:, None] * inv[None, :]
    cos, sin = jnp.cos(ang), jnp.sin(ang)
    ones = jnp.ones((seq, QK_NOPE_DIM), F32)
    zeros = jnp.zeros((seq, QK_NOPE_DIM), F32)
    pad1 = jnp.ones((seq, HEAD_PAD - QK_DIM), F32)
    pad0 = jnp.zeros((seq, HEAD_PAD - QK_DIM), F32)
    zr = jnp.zeros_like(sin)
    ctab = jnp.concatenate([ones, cos, cos, pad1], axis=1)
    s1tab = jnp.concatenate([zeros, -sin, zr, pad0], axis=1)
    s2tab = jnp.concatenate([zeros, zr, sin, pad0], axis=1)
    return ctab, s1tab, s2tab


def _post_kernel(ot_ref, yf_ref, gates_ref, x_ref, mod_ref, wa_ref, wb_ref, wout_ref, gffn_ref,
                 wrh_ref, wrl_ref, br_ref, x1_ref, h2p_ref, ti_ref, tg_ref, *, d):
    tm = x_ref.shape[0]
    ya = _dot_tn(ot_ref[...], wa_ref[...])
    yb = _dot(yf_ref[...], wb_ref[...])
    merged = gates_ref[:, 0:d].astype(F32) * ya + gates_ref[:, d:2 * d].astype(F32) * yb
    x1 = x_ref[...] + mod_ref[2:3, :] * _dot(merged.astype(BF16), wout_ref[...])
    x1_ref[...] = x1
    h2 = (_rms(x1) * gffn_ref[...]) * (1.0 + mod_ref[4:5, :]) + mod_ref[3:4, :]

    h_hi, h_lo = _split_bf16(h2)
    logits = _dot(h_hi, wrh_ref[...]) + _dot(h_hi, wrl_ref[...]) + _dot(h_lo, wrh_ref[...]) + br_ref[...]
    lane = lax.broadcasted_iota(jnp.int32, logits.shape, 1)
    work = logits
    vals, idxs = [], []
    for _ in range(TOP_K):
        mx = jnp.max(work, axis=1, keepdims=True)
        ix = jnp.min(jnp.where(work == mx, lane, LANES), axis=1, keepdims=True)
        vals.append(mx)
        idxs.append(ix)
        work = jnp.where(lane == ix, -jnp.inf, work)
    exps = [jnp.exp(v - vals[0]) for v in vals]
    tot = exps[0]
    for e in exps[1:]:
        tot = tot + e
    ti = jnp.zeros(logits.shape, jnp.int32)
    tg = jnp.zeros(logits.shape, F32)
    for kk in range(TOP_K):
        ti = jnp.where(lane == kk, idxs[kk], ti)
        tg = jnp.where(lane == kk, exps[kk] / tot, tg)
    ti_ref[...] = ti[:, 0:TOPK_PAD]
    tg_ref[...] = tg[:, 0:TOPK_PAD]

    bits = lax.bitcast_convert_type(h2.astype(BF16).astype(F32), jnp.uint32)
    half = d // 2
    h2p_ref[...] = _pack_rows((bits[:, 0:half] >> 16) | (bits[:, half:d] & jnp.uint32(0xFFFF0000)))


def _post_mix(ot, yf, gates, x, mod, wa, wb, wout, gffn, wrh, wrl, br, *, bsz, seq):
    n_tok, d = x.shape
    tm, _, _, _ = _tiles(seq, n_tok)
    per_b = seq // tm
    hv = N_HEADS * V_HEAD_DIM
    nchp = d // 2 // LANES

    def tok(i):
        return (i, 0)

    def const2(i):
        return (0, 0)

    return pl.pallas_call(
        functools.partial(_post_kernel, d=d),
        grid=(n_tok // tm,),
        in_specs=[
            pl.BlockSpec((hv, tm), lambda i: (0, i)),
            pl.BlockSpec((tm, F_WIDTH), tok),
            pl.BlockSpec((tm, 2 * d), tok),
            pl.BlockSpec((tm, d), tok),
            pl.BlockSpec((None, N_MOD, d), lambda i: (i // per_b, 0, 0)),
            pl.BlockSpec((hv, d), const2),
            pl.BlockSpec((F_WIDTH, d), const2),
            pl.BlockSpec((d, d), const2),
            pl.BlockSpec((1, d), const2),
            pl.BlockSpec((d, LANES), const2),
            pl.BlockSpec((d, LANES), const2),
            pl.BlockSpec((1, LANES), const2),
        ],
        out_specs=[
            pl.BlockSpec((tm, d), tok),
            pl.BlockSpec((tm * nchp, LANES), tok),
            pl.BlockSpec((tm, TOPK_PAD), tok),
            pl.BlockSpec((tm, TOPK_PAD), tok),
        ],
        out_shape=[
            jax.ShapeDtypeStruct((n_tok, d), F32),
            jax.ShapeDtypeStruct((n_tok * nchp, LANES), jnp.uint32),
            jax.ShapeDtypeStruct((n_tok, TOPK_PAD), jnp.int32),
            jax.ShapeDtypeStruct((n_tok, TOPK_PAD), F32),
        ],
        compiler_params=_cparams(("arbitrary",)),
        name="post_mix",
    )(ot, yf, gates, x, mod, wa, wb, wout, gffn, wrh, wrl, br)


def _moe_kernel(bexp_ref, nv_ref, tab_ref, h2p_ref, wgu_ref, wd_ref, bias_ref,
                out_ref, xa_ref, xb_ref, ya_ref, yb_ref, *, tc):
    c = pl.program_id(0)
    j = pl.program_id(1)
    nchp = h2p_ref.shape[0] // tc
    nc = out_ref.shape[0] // tc
    rows = MOE_ROWS
    gp = nchp * SUBLANES
    go = nc * SUBLANES

    @pl.when(j == 0)
    def _():
        out_ref[...] = jnp.zeros_like(out_ref)
        xb_ref[...] = jnp.zeros_like(xb_ref)
        ya_ref[...] = jnp.zeros_like(ya_ref)
        yb_ref[...] = jnp.zeros_like(yb_ref)

    def step(x_nxt, x_cur, y_cur, y_prv):
        for m in range(rows):
            t = tab_ref[0, 0, m]
            x_nxt[pl.ds((m >> 3) * gp + (m & 7), nchp, stride=SUBLANES), :] = (
                h2p_ref[pl.ds((t >> 3) * gp + (t & 7), nchp, stride=SUBLANES), :])
        xs4 = x_cur[...].reshape(rows // SUBLANES, nchp, SUBLANES, LANES)
        los, his = [], []
        for cc in range(nchp):
            wd = xs4[:, cc].reshape(rows, LANES)
            los.append(lax.bitcast_convert_type(wd << 16, F32))
            his.append(lax.bitcast_convert_type(wd & jnp.uint32(0xFFFF0000), F32))
        x = jnp.concatenate(los + his, axis=1).astype(BF16)
        ff = wd_ref.shape[0]
        g = jnp.minimum(_dot(x, wgu_ref[:, 0:ff]) + bias_ref[:, 0:ff], SWIGLU_LIMIT)
        u = jnp.clip(_dot(x, wgu_ref[:, ff:2 * ff]) + bias_ref[:, ff:2 * ff], -SWIGLU_LIMIT, SWIGLU_LIMIT)
        act = ((u + 1.0) * (g * jax.nn.sigmoid(g * SWIGLU_ALPHA))).astype(BF16)
        y_cur[...] = _pack_rows(_dot(act, wd_ref[...]) + bias_ref[:, 2 * ff:])
        for m0 in range(0, rows, SCATTER_BATCH):
            pend = []
            for m in range(m0, m0 + SCATTER_BATCH):
                t = tab_ref[0, 1, m]
                gate = lax.bitcast_convert_type(tab_ref[0, 2, m], F32)
                base = (t >> 3) * go + (t & 7)
                rv = y_prv[pl.ds((m >> 3) * go + (m & 7), nc, stride=SUBLANES), :]
                pend.append((base, out_ref[pl.ds(base, nc, stride=SUBLANES), :] + rv * gate))
            for base, val in pend:
                out_ref[pl.ds(base, nc, stride=SUBLANES), :] = val

    live = j < nv_ref[c] + 2

    @pl.when(jnp.logical_and(live, j % 2 == 0))
    def _():
        step(xa_ref, xb_ref, yb_ref, ya_ref)

    @pl.when(jnp.logical_and(live, j % 2 == 1))
    def _():
        step(xb_ref, xa_ref, ya_ref, yb_ref)


def _moe_ffn(h2p, top_i, top_g, wgu, wd, bias, *, d):
    n_tok = top_i.shape[0]
    nchp = h2p.shape[0] // n_tok
    _, _, _, tc = _tiles(n_tok, n_tok)
    nc = d // LANES
    rows = MOE_ROWS
    n_chunks = n_tok // tc
    n_asg = tc * TOP_K
    n_blk = n_asg // rows + N_EXPERTS
    n_exp, ff, _ = wd.shape

    e_flat = top_i.reshape(n_chunks, n_asg)
    g_flat = top_g.reshape(n_chunks, n_asg)
    key = e_flat * n_asg + jnp.arange(n_asg, dtype=jnp.int32)[None, :]
    key = jnp.sort(key, axis=1)
    order = key % n_asg
    counts = jnp.sum((e_flat[:, :, None] == jnp.arange(n_exp, dtype=jnp.int32)[None, None, :]).astype(jnp.int32),
                     axis=1)
    start = jnp.cumsum(counts, axis=1) - counts
    nb_e = (counts + rows - 1) // rows
    blk_end = jnp.cumsum(nb_e, axis=1)
    blk_start = blk_end - nb_e
    jb = jnp.arange(n_blk, dtype=jnp.int32)
    e_of_blk = jnp.sum((jb[None, :, None] >= blk_end[:, None, :]).astype(jnp.int32), axis=2)
    blk_valid = e_of_blk < n_exp
    last_e = jnp.max(jnp.where(counts > 0, jnp.arange(n_exp, dtype=jnp.int32)[None, :], 0), axis=1, keepdims=True)
    e_blk = jnp.where(blk_valid, jnp.minimum(e_of_blk, n_exp - 1), last_e)
    q0 = (jb[None, :] - jnp.take_along_axis(blk_start, e_blk, axis=1)) * rows
    cnt_blk = jnp.take_along_axis(counts, e_blk, axis=1)
    n_valid = jnp.where(blk_valid, jnp.clip(cnt_blk - q0, 0, rows), 0)
    r = jnp.arange(rows, dtype=jnp.int32)
    q = q0[:, :, None] + r[None, None, :]
    row_ok = r[None, None, :] < n_valid[:, :, None]
    s_idx = jnp.clip(jnp.take_along_axis(start, e_blk, axis=1)[:, :, None] + q, 0, n_asg - 1)
    a_idx = jnp.take_along_axis(order, s_idx.reshape(n_chunks, n_blk * rows), axis=1)
    row_tok = jnp.where(row_ok.reshape(n_chunks, -1), a_idx // TOP_K, 0).astype(jnp.int32)
    row_gate = jnp.where(row_ok.reshape(n_chunks, -1), jnp.take_along_axis(g_flat, a_idx, axis=1), 0.0)
    n_steps = n_blk + 2
    row_tok = row_tok.reshape(n_chunks, n_blk, rows)
    row_gate = row_gate.reshape(n_chunks, n_blk, rows).astype(F32)
    pad_i = jnp.zeros((n_chunks, 2, rows), jnp.int32)
    gate_bits = lax.bitcast_convert_type(row_gate, jnp.int32)
    table = jnp.stack([jnp.concatenate([row_tok, pad_i], axis=1),
                       jnp.concatenate([pad_i, row_tok], axis=1),
                       jnp.concatenate([pad_i, gate_bits], axis=1)], axis=2)
    table = table.reshape(n_chunks * n_steps, 3, rows)
    e_blk = e_blk.astype(jnp.int32)
    bexp = jnp.concatenate([e_blk[:, :1], e_blk, e_blk[:, -1:]], axis=1).reshape(-1)
    n_live = blk_end[:, -1].astype(jnp.int32)

    def wmap(c, j, bexp_ref, nv_ref):
        return (bexp_ref[c * n_steps + j], 0, 0)

    def rowmap(c, j, bexp_ref, nv_ref):
        return (c * n_steps + j, 0, 0)

    def chunkmap(c, j, bexp_ref, nv_ref):
        return (c, 0)

    grid_spec = pltpu.PrefetchScalarGridSpec(
        num_scalar_prefetch=2,
        grid=(n_chunks, n_steps),
        in_specs=[
            pl.BlockSpec((1, 3, rows), rowmap, memory_space=pltpu.SMEM),
            pl.BlockSpec((tc * nchp, LANES), chunkmap, pipeline_mode=pl.Buffered(1)),
            pl.BlockSpec((None, d, 2 * ff), wmap),
            pl.BlockSpec((None, ff, d), wmap),
            pl.BlockSpec((None, 1, 2 * ff + d), wmap),
        ],
        out_specs=pl.BlockSpec((tc * nc, LANES), chunkmap),
        scratch_shapes=[
            pltpu.VMEM((rows * nchp, LANES), jnp.uint32),
            pltpu.VMEM((rows * nchp, LANES), jnp.uint32),
            pltpu.VMEM((rows * nc, LANES), F32),
            pltpu.VMEM((rows * nc, LANES), F32),
        ],
    )
    return pl.pallas_call(
        functools.partial(_moe_kernel, tc=tc),
        grid_spec=grid_spec,
        out_shape=jax.ShapeDtypeStruct((n_tok * nc, LANES), F32),
        compiler_params=_cparams(("arbitrary", "arbitrary")),
        name="moe_ffn",
    )(bexp, n_live, table, h2p, wgu, wd, bias)


def _final_kernel(x_ref, moe_ref, mod_ref, g_ref, o_ref):
    tm = x_ref.shape[0]
    x = x_ref[...] + mod_ref[5:6, :] * _unpack_rows(moe_ref, tm)
    o_ref[...] = _rms(x) * g_ref[...]


def _final_norm(x, moe, mod, g_final, *, bsz, seq):
    n_tok, d = x.shape
    tm, _, _, _ = _tiles(seq, n_tok)
    per_b = seq // tm
    nch = d // LANES
    return pl.pallas_call(
        _final_kernel,
        grid=(n_tok // tm,),
        in_specs=[
            pl.BlockSpec((tm, d), lambda i: (i, 0)),
            pl.BlockSpec((tm * nch, LANES), lambda i: (i, 0)),
            pl.BlockSpec((None, N_MOD, d), lambda i: (i // per_b, 0, 0)),
            pl.BlockSpec((1, d), lambda i: (0, 0)),
        ],
        out_specs=pl.BlockSpec((tm, d), lambda i: (i, 0)),
        out_shape=jax.ShapeDtypeStruct((n_tok, d), F32),
        compiler_params=_cparams(("arbitrary",)),
        name="final_norm",
    )(x, moe, mod, g_final)


def _deinterleave_kernel(w_ref, p_ref, o_ref):
    ff = w_ref.shape[1] // 2
    for kg in range(ff // LANES):
        blk = w_ref[:, kg * 2 * LANES:(kg + 1) * 2 * LANES].astype(BF16)
        sel = _dot(blk, p_ref[...])
        o_ref[:, kg * LANES:(kg + 1) * LANES] = sel[:, 0:LANES].astype(BF16)
        o_ref[:, ff + kg * LANES:ff + (kg + 1) * LANES] = sel[:, LANES:2 * LANES].astype(BF16)


def _deinterleave_gate_up(w_gu):
    n_layers, n_exp, d, ff2 = w_gu.shape
    rows = n_layers * n_exp * d
    tr = min(PREP_ROW_TILE, rows)
    sel = np.zeros((2 * LANES, 2 * LANES), np.float32)
    sel[2 * np.arange(LANES), np.arange(LANES)] = 1.0
    sel[2 * np.arange(LANES) + 1, LANES + np.arange(LANES)] = 1.0
    out = pl.pallas_call(
        _deinterleave_kernel,
        grid=(rows // tr,),
        in_specs=[pl.BlockSpec((tr, ff2), lambda i: (i, 0)),
                  pl.BlockSpec((2 * LANES, 2 * LANES), lambda i: (0, 0))],
        out_specs=pl.BlockSpec((tr, ff2), lambda i: (i, 0)),
        out_shape=jax.ShapeDtypeStruct((rows, ff2), BF16),
        compiler_params=_cparams(("arbitrary",)),
        name="deinterleave_gate_up",
    )(w_gu.reshape(rows, ff2), jnp.asarray(sel, BF16))
    return out.reshape(n_layers, n_exp, d, ff2)


def _prep_weights(w_in, w_uq, w_ukv, w_a, w_b, w_out, w_router, b_router, w_gu, b_gu, w_dn, b_dn):
    n_layers, d, _ = w_in.shape
    o1 = Q_LORA_RANK
    o2 = o1 + KV_LORA_RANK
    o3 = o2 + QK_ROPE_DIM
    kr = jnp.zeros((n_layers, d, KR_PAD), w_in.dtype).at[:, :, QK_NOPE_DIM:QK_DIM].set(w_in[:, :, o2:o3])
    win = jnp.concatenate([w_in[:, :, :o2], kr, w_in[:, :, o3:]], axis=2).astype(BF16)
    wuq = jnp.pad(w_uq.reshape(n_layers, Q_LORA_RANK, N_HEADS, QK_DIM),
                  ((0, 0), (0, 0), (0, 0), (0, HEAD_PAD - QK_DIM)))
    wuq = wuq.reshape(n_layers, Q_LORA_RANK, N_HEADS * HEAD_PAD).astype(BF16)
    wkv = w_ukv.reshape(n_layers, KV_LORA_RANK, N_HEADS, QK_NOPE_DIM + V_HEAD_DIM)
    wkn = jnp.pad(wkv[..., :QK_NOPE_DIM], ((0, 0), (0, 0), (0, 0), (0, HEAD_PAD - QK_NOPE_DIM)))
    wkn = wkn.reshape(n_layers, KV_LORA_RANK, N_HEADS * HEAD_PAD).astype(BF16)
    wv = wkv[..., QK_NOPE_DIM:].reshape(n_layers, KV_LORA_RANK, N_HEADS * V_HEAD_DIM).astype(BF16)
    n_exp = w_router.shape[2]
    wr = jnp.pad(w_router, ((0, 0), (0, 0), (0, LANES - n_exp)))
    wr_hi = wr.astype(BF16)
    wr_lo = (wr - wr_hi.astype(F32)).astype(BF16)
    br = jnp.pad(b_router, ((0, 0), (0, LANES - n_exp)), constant_values=NEG_BIG)[:, None, :]
    wgu = _deinterleave_gate_up(w_gu)
    bias = jnp.concatenate([b_gu[..., 0::2], b_gu[..., 1::2], b_dn], axis=-1)[:, :, None, :]
    wd = w_dn.astype(BF16)
    return dict(win=win, wuq=wuq, wkn=wkn, wv=wv, wa=w_a.astype(BF16), wb=w_b.astype(BF16),
                wout=w_out.astype(BF16), wr_hi=wr_hi, wr_lo=wr_lo, br=br, wgu=wgu, wd=wd, bias=bias)


def _trunk(x3, mod, wts, g_mix, g_ffn, g_q, g_kv, g_final, bdc, bds):
    bsz, seq, d = x3.shape
    n_layers = g_mix.shape[0]
    x = x3.reshape(bsz * seq, d)
    tabs = _rope_tabs(seq)
    dft = _dft_consts(seq)
    moe = None
    for l in range(n_layers):
        mod_prev = mod[l - 1] if l > 0 else None
        q, k, vt, w, gates, x = _in_proj(
            x, moe, mod_prev, mod[l], g_mix[l][None, :], wts["win"][l], g_q[l][None, :], wts["wuq"][l],
            g_kv[l][None, :], wts["wkn"][l], wts["wv"][l], tabs, bdc, bds, bsz=bsz, seq=seq)
        ot = _attention(q, k, vt, bsz=bsz, seq=seq)
        yf = _seq_dft_real(w, dft, bsz=bsz, seq=seq)
        x, h2p, top_i, top_g = _post_mix(
            ot, yf, gates, x, mod[l], wts["wa"][l], wts["wb"][l], wts["wout"][l], g_ffn[l][None, :],
            wts["wr_hi"][l], wts["wr_lo"][l], wts["br"][l], bsz=bsz, seq=seq)
        moe = _moe_ffn(h2p, top_i[:, :TOP_K], top_g[:, :TOP_K], wts["wgu"][l], wts["wd"][l], wts["bias"][l], d=d)
    y = _final_norm(x, moe, mod[n_layers - 1], g_final[None, :], bsz=bsz, seq=seq)
    return y.reshape(bsz, seq, d)


def kernel(x_prompt, x_sample, c_prompt, c_sample, w_ada, b_ada, g_mix, g_ffn, w_in, g_q, w_uq, g_kv, w_ukv, w_a, w_b, w_out, w_router, b_router, w_gu, b_gu, w_dn, b_dn, g_final):
    n_layers, d, _ = w_ada.shape
    bp, bs = c_prompt.shape[0], c_sample.shape[0]
    rows = -(-(bp + bs) // SUBLANES) * SUBLANES
    c_all = jnp.concatenate([c_prompt, c_sample, jnp.zeros((rows - bp - bs, d), c_prompt.dtype)], axis=0)
    mod = _adaln_mod(c_all, w_ada, b_ada).reshape(n_layers, rows, N_MOD, d)
    wts = _prep_weights(w_in, w_uq, w_ukv, w_a, w_b, w_out, w_router, b_router, w_gu, b_gu, w_dn, b_dn)
    bdc, bds = _channel_dft_consts()
    y_prompt = _trunk(x_prompt, mod[:, :bp], wts, g_mix, g_ffn, g_q, g_kv, g_final, bdc, bds)
    y_sample = _trunk(x_sample, mod[:, bp:bp + bs], wts, g_mix, g_ffn, g_q, g_kv, g_final, bdc, bds)
    return (y_prompt, y_sample)
```
